```python
import math
import jax, jax.numpy as jnp
from jax import lax
import numpy as np

D_MODEL = 1024
BATCH = 2
SEQ = 8192
DEPTH = 4
DEC_BATCH = 16
DEC_SEQ = 16
PAST_LEN = 2048

CHUNK = 64
N_A = DEPTH // 2
N_B = DEPTH - N_A
N_HEADS = 16
HEAD_DIM = D_MODEL // N_HEADS
CONV_W = 3
D_FF = 2816
Q_BLOCK = 128
LN_EPS = 1e-5
DN_ALPHA = (2.0 * DEPTH) ** 0.25
DN_BETA = (8.0 * DEPTH) ** -0.25

kernel_name = "yoco_shortconv_stickbreaking_streaming_step"


def layer_norm(x, g, b):
    xf = x.astype(jnp.float32)
    mu = jnp.mean(xf, axis=-1, keepdims=True)
    var = jnp.mean(jnp.square(xf - mu), axis=-1, keepdims=True)
    y = (xf - mu) * lax.rsqrt(var + LN_EPS) * g.astype(jnp.float32) + b.astype(jnp.float32)
    return y.astype(x.dtype)


def swiglu(x, wg, wu, wd):
    return (jax.nn.silu(x @ wg) * (x @ wu)) @ wd


def short_conv_mixer(x, prev, w_in, w_conv, w_out):
    T = x.shape[1]
    gb, gc, h = jnp.split(x @ w_in, 3, axis=-1)
    u = gc * h
    up = jnp.concatenate([prev.astype(u.dtype), u], axis=1)
    conv = sum(w_conv[i] * up[:, i:i + T] for i in range(CONV_W))
    y = (gb * conv) @ w_out
    return y, up[:, -(CONV_W - 1):]


def stick_breaking(q, k, v, q_pos, k_pos):
    z = jnp.einsum('bqhd,bkhd->bhqk', q, k).astype(jnp.float32) * (HEAD_DIM ** -0.5)
    vis = k_pos[None, :] < q_pos[:, None]
    log_keep = jnp.where(vis, jax.nn.log_sigmoid(-z), 0.0)
    after = lax.cumsum(log_keep, axis=3, reverse=True) - log_keep
    w = jnp.where(vis, jnp.exp(jax.nn.log_sigmoid(z) + after), 0.0)
    o = jnp.einsum('bhqk,bkhd->bqhd', w, v.astype(jnp.float32))
    return o.astype(q.dtype)


def sb_prompt(q, k, v):
    B, T, H, Dh = q.shape
    nblk = T // Q_BLOCK
    q_blocks = q.reshape(B, nblk, Q_BLOCK, H, Dh).transpose(1, 0, 2, 3, 4)
    pos = jnp.arange(T, dtype=jnp.int32)
    pos_blocks = pos.reshape(nblk, Q_BLOCK)
    out = lax.map(lambda a: stick_breaking(a[0], k, v, a[1], pos), (q_blocks, pos_blocks))
    return out.transpose(1, 0, 2, 3, 4).reshape(B, T, H, Dh)


def trunk(x, conv_prev, cache_k, cache_v, ln_g, ln_b, w_ffn_gate, w_ffn_up, w_ffn_down,
          w_conv_in, w_conv, w_conv_out, w_kv, w_q, w_o):
    B, T, _ = x.shape
    conv_states = []
    k_new = v_new = k_all = v_all = None
    for l in range(DEPTH):
        x = layer_norm(DN_ALPHA * x + 0.5 * swiglu(x, w_ffn_gate[l, 0], w_ffn_up[l, 0], w_ffn_down[l, 0]),
                       ln_g[l, 0], ln_b[l, 0])
        if l < N_A:
            y, st = short_conv_mixer(x, conv_prev[l], w_conv_in[l], w_conv[l], w_conv_out[l])
            conv_states.append(st)
        else:
            i = l - N_A
            q = (x @ w_q[i]).reshape(B, T, N_HEADS, HEAD_DIM)
            if cache_k is None:
                o = sb_prompt(q, k_all, v_all)
            else:
                P = cache_k.shape[1]
                q_pos = P + jnp.arange(T, dtype=jnp.int32)
                k_pos = jnp.arange(P + T, dtype=jnp.int32)
                o = stick_breaking(q, k_all, v_all, q_pos, k_pos)
            y = o.reshape(B, T, D_MODEL) @ w_o[i]
        x = layer_norm(DN_ALPHA * x + y, ln_g[l, 1], ln_b[l, 1])
        x = layer_norm(DN_ALPHA * x + 0.5 * swiglu(x, w_ffn_gate[l, 1], w_ffn_up[l, 1], w_ffn_down[l, 1]),
                       ln_g[l, 2], ln_b[l, 2])
        if l == N_A - 1:
            k_new, v_new = jnp.split(x @ w_kv, 2, axis=-1)
            k_new = k_new.reshape(B, T, N_HEADS, HEAD_DIM)
            v_new = v_new.reshape(B, T, N_HEADS, HEAD_DIM)
            if cache_k is None:
                k_all, v_all = k_new, v_new
            else:
                k_all = jnp.concatenate([cache_k.astype(k_new.dtype), k_new], axis=1)
                v_all = jnp.concatenate([cache_v.astype(v_new.dtype), v_new], axis=1)
    return x, k_new, v_new, jnp.stack(conv_states, axis=0)


def setup_inputs(seed: int = 0) -> dict:
    key = jax.random.key(seed)
    ks = jax.random.split(key, 16)
    f32 = jnp.float32
    nrm = lambda k, shape, s: jax.random.normal(k, shape, f32) * s
    x_prompt = nrm(ks[0], (BATCH, SEQ, D_MODEL), 1.0)
    x_sample = nrm(ks[1], (DEC_BATCH, DEC_SEQ, D_MODEL), 1.0)
    cache_k = nrm(ks[2], (DEC_BATCH, PAST_LEN, N_HEADS, HEAD_DIM), 1.0)
    cache_v = nrm(ks[3], (DEC_BATCH, PAST_LEN, N_HEADS, HEAD_DIM), DN_BETA)
    state_conv = nrm(ks[4], (N_A, DEC_BATCH, CONV_W - 1, D_MODEL), 1.0)
    ln_g = 1.0 + nrm(ks[5], (DEPTH, 3, D_MODEL), 0.01)
    ln_b = nrm(ks[6], (DEPTH, 3, D_MODEL), 0.01)
    w_ffn_gate = nrm(ks[7], (DEPTH, 2, D_MODEL, D_FF), D_MODEL ** -0.5)
    w_ffn_up = nrm(ks[8], (DEPTH, 2, D_MODEL, D_FF), D_MODEL ** -0.5)
    w_ffn_down = nrm(ks[9], (DEPTH, 2, D_FF, D_MODEL), DN_BETA * D_FF ** -0.5)
    w_conv_in = nrm(ks[10], (N_A, D_MODEL, 3 * D_MODEL), D_MODEL ** -0.5)
    w_conv = nrm(ks[11], (N_A, CONV_W, D_MODEL), CONV_W ** -0.5)
    w_conv_out = nrm(ks[12], (N_A, D_MODEL, D_MODEL), DN_BETA * D_MODEL ** -0.5)
    kk, kv = jax.random.split(ks[13])
    w_kv = jnp.concatenate([nrm(kk, (D_MODEL, D_MODEL), D_MODEL ** -0.5),
                            nrm(kv, (D_MODEL, D_MODEL), DN_BETA * D_MODEL ** -0.5)], axis=-1)
    w_q = nrm(ks[14], (N_B, D_MODEL, D_MODEL), D_MODEL ** -0.5)
    w_o = nrm(ks[15], (N_B, D_MODEL, D_MODEL), DN_BETA * D_MODEL ** -0.5)
    return {"x_prompt": x_prompt, "x_sample": x_sample, "cache_k": cache_k, "cache_v": cache_v,
            "state_conv": state_conv, "ln_g": ln_g, "ln_b": ln_b, "w_ffn_gate": w_ffn_gate,
            "w_ffn_up": w_ffn_up, "w_ffn_down": w_ffn_down, "w_conv_in": w_conv_in, "w_conv": w_conv,
            "w_conv_out": w_conv_out, "w_kv": w_kv, "w_q": w_q, "w_o": w_o}


def reference(x_prompt, x_sample, cache_k, cache_v, state_conv, ln_g, ln_b, w_ffn_gate, w_ffn_up,
              w_ffn_down, w_conv_in, w_conv, w_conv_out, w_kv, w_q, w_o):
    weights = (ln_g, ln_b, w_ffn_gate, w_ffn_up, w_ffn_down, w_conv_in, w_conv, w_conv_out, w_kv, w_q, w_o)
    zero_prev = jnp.zeros((N_A, x_prompt.shape[0], CONV_W - 1, D_MODEL), x_prompt.dtype)
    y_prompt, k_prompt, v_prompt, conv_prompt = trunk(x_prompt, zero_prev, None, None, *weights)
    y_sample, k_sample, v_sample, conv_sample = trunk(x_sample, state_conv, cache_k, cache_v, *weights)
    return (y_prompt, y_sample, k_prompt, v_prompt, conv_prompt, k_sample, v_sample, conv_sample)
```

```python
import functools

import jax
import jax.numpy as jnp
from jax import lax
from jax.experimental import pallas as pl
from jax.experimental.pallas import tpu as pltpu

F32 = jnp.float32
BF16 = jnp.bfloat16

N_HEADS = 16
CONV_W = 3
LN_EPS = 1e-5

V7X_LANES = 128
V7X_VMEM_BYTES = 64 * 1024 * 1024
VMEM_LIMIT_BYTES = 56 * 1024 * 1024

TM = 256
TQ = 128
TK = 128
TM_CONV = 512

NEG_CUT = -110.0
ALL_VISIBLE = 1 << 24


def _dot(a, b):
    return jnp.dot(a, b, preferred_element_type=F32)


def _layer_norm(r, g, b):
    mu = jnp.mean(r, axis=-1, keepdims=True)
    xc = r - mu
    var = jnp.mean(xc * xc, axis=-1, keepdims=True)
    return xc * lax.rsqrt(var + LN_EPS) * g + b


def _params(*sem):
    return pltpu.CompilerParams(dimension_semantics=sem, vmem_limit_bytes=VMEM_LIMIT_BYTES)


def _const_spec(shape):
    nd = len(shape)
    return pl.BlockSpec(shape, lambda *_: (0,) * nd, pipeline_mode=pl.Buffered(1))


def _ffn_ln_kernel(x_ref, wg_ref, wu_ref, wd_ref, g_ref, b_ref, o_ref, *, alpha):
    x = x_ref[...]
    xb = x.astype(BF16)
    gate = _dot(xb, wg_ref[...])
    up = _dot(xb, wu_ref[...])
    h = (gate * jax.nn.sigmoid(gate) * up).astype(BF16)
    y = _dot(h, wd_ref[...])
    o_ref[...] = _layer_norm(alpha * x + 0.5 * y, g_ref[...], b_ref[...])


def _ffn_ln(x, wg, wu, wd, g, b, alpha):
    n, d = x.shape
    dff = wg.shape[1]
    return pl.pallas_call(
        functools.partial(_ffn_ln_kernel, alpha=alpha),
        grid=(n // TM,),
        in_specs=[pl.BlockSpec((TM, d), lambda i: (i, 0)),
                  _const_spec((d, dff)), _const_spec((d, dff)), _const_spec((dff, d)),
                  _const_spec((1, d)), _const_spec((1, d))],
        out_specs=pl.BlockSpec((TM, d), lambda i: (i, 0)),
        out_shape=jax.ShapeDtypeStruct((n, d), F32),
        compiler_params=_params("arbitrary"),
        name="ffn_ln",
    )(x, wg, wu, wd, g, b)


def _conv_ln_kernel(x_ref, prev_ref, win_ref, wconv_ref, wout_ref, g_ref, b_ref,
                    o_ref, st_ref, car_ref, *, alpha):
    tm, d = x_ref.shape

    @pl.when(pl.program_id(1) == 0)
    def _():
        car_ref[...] = prev_ref[0]

    x = x_ref[...]
    p = _dot(x.astype(BF16), win_ref[...])
    gate_out = p[:, :d]
    u = p[:, d:2 * d] * p[:, 2 * d:]
    prev2 = car_ref[0:1, :]
    prev1 = car_ref[1:2, :]
    row = lax.broadcasted_iota(jnp.int32, (tm, d), 0)
    u1 = jnp.where(row == 0, prev1, pltpu.roll(u, 1, 0))
    u2 = jnp.where(row == 0, prev2, jnp.where(row == 1, prev1, pltpu.roll(u, 2, 0)))
    wc = wconv_ref[...]
    conv = wc[0:1, :] * u2 + wc[1:2, :] * u1 + wc[2:3, :] * u
    y = _dot((gate_out * conv).astype(BF16), wout_ref[...])
    o_ref[...] = _layer_norm(alpha * x + y, g_ref[...], b_ref[...])
    last = u[tm - (CONV_W - 1):, :]
    car_ref[...] = last
    st_ref[0] = last


def _conv_ln(x, prev, win, wconv, wout, g, b, alpha, *, row0, streams, t_len, tm):
    n, d = x.shape
    nt = t_len // tm
    blk0 = row0 // tm
    x_spec = pl.BlockSpec((tm, d), lambda s, t: (blk0 + s * nt + t, 0))
    st_spec = pl.BlockSpec((1, CONV_W - 1, d), lambda s, t: (s, 0, 0))
    return pl.pallas_call(
        functools.partial(_conv_ln_kernel, alpha=alpha),
        grid=(streams, nt),
        in_specs=[x_spec, st_spec,
                  _const_spec(win.shape), _const_spec(wconv.shape), _const_spec(wout.shape),
                  _const_spec((1, d)), _const_spec((1, d))],
        out_specs=[x_spec, st_spec],
        out_shape=[jax.ShapeDtypeStruct((n, d), F32),
                   jax.ShapeDtypeStruct((streams, CONV_W - 1, d), F32)],
        scratch_shapes=[pltpu.VMEM((CONV_W - 1, d), F32)],
        input_output_aliases={0: 0},
        compiler_params=_params("arbitrary", "arbitrary"),
        name="conv_ln",
    )(x, prev, win, wconv, wout, g, b)


def _kv_kernel(x_ref, wk_ref, wv_ref, kp_ref, vp_ref, ks_ref, vs_ref, kt_ref, vb_ref, *, n_prompt_tiles):
    i = pl.program_id(0)
    xb = x_ref[...].astype(BF16)
    k = _dot(xb, wk_ref[...])
    v = _dot(xb, wv_ref[...])

    @pl.when(i < n_prompt_tiles)
    def _():
        kp_ref[...] = k
        vp_ref[...] = v
        vb_ref[...] = v.astype(BF16)
        for hp in range(kt_ref.shape[1]):
            for c in range(kt_ref.shape[2]):
                blk = k[c * TK:(c + 1) * TK, hp * V7X_LANES:(hp + 1) * V7X_LANES]
                kt_ref[0, hp, c] = blk.T.astype(BF16)

    @pl.when(i >= n_prompt_tiles)
    def _():
        ks_ref[...] = k
        vs_ref[...] = v


def _kv_proj(x, wk, wv, *, n_prompt, batch):
    n, d = x.shape
    n_sample = n - n_prompt
    assert n_sample == TM
    npt = n_prompt // TM
    tiles_per_stream = npt // batch
    n_hp = d // V7X_LANES
    last = npt - 1
    prompt_spec = pl.BlockSpec((TM, d), lambda i: (jnp.minimum(i, last), 0))
    sample_spec = pl.BlockSpec((TM, d), lambda i: (0, 0))
    kt_spec = pl.BlockSpec(
        (1, n_hp, TM // TK, V7X_LANES, TK),
        lambda i: (jnp.minimum(i, last) // tiles_per_stream, 0, jnp.minimum(i, last) % tiles_per_stream, 0, 0))
    return pl.pallas_call(
        functools.partial(_kv_kernel, n_prompt_tiles=npt),
        grid=(n // TM,),
        in_specs=[pl.BlockSpec((TM, d), lambda i: (i, 0)), _const_spec((d, d)), _const_spec((d, d))],
        out_specs=[prompt_spec, prompt_spec, sample_spec, sample_spec, kt_spec, prompt_spec],
        out_shape=[jax.ShapeDtypeStruct((n_prompt, d), F32), jax.ShapeDtypeStruct((n_prompt, d), F32),
                   jax.ShapeDtypeStruct((n_sample, d), F32), jax.ShapeDtypeStruct((n_sample, d), F32),
                   jax.ShapeDtypeStruct((batch, n_hp, n_prompt // batch // TK, V7X_LANES, TK), BF16),
                   jax.ShapeDtypeStruct((n_prompt, d), BF16)],
        compiler_params=_params("arbitrary"),
        name="kv_proj",
    )(x, wk, wv)


def _q_kernel(x_ref, wq_ref, q_ref, *, scale):
    q_ref[...] = (_dot(x_ref[...].astype(BF16), wq_ref[...]) * scale).astype(q_ref.dtype)


def _q_proj(x, wq, scale):
    n, d = x.shape
    return pl.pallas_call(
        functools.partial(_q_kernel, scale=scale),
        grid=(n // TM,),
        in_specs=[pl.BlockSpec((TM, d), lambda i: (i, 0)), _const_spec((d, d))],
        out_specs=pl.BlockSpec((TM, d), lambda i: (i, 0)),
        out_shape=jax.ShapeDtypeStruct((n, d), BF16),
        compiler_params=_params("arbitrary"),
        name="q_proj",
    )(x, wq)


def _sb_block(z, vis, v_blk, umat, acc_ref, car_ref):
    tk = z.shape[1]
    log_keep = jnp.where(vis, -(jnp.maximum(z, 0.0) + jnp.log1p(jnp.exp(-jnp.abs(z)))), 0.0)
    hi = log_keep.astype(BF16)
    lo = (log_keep - hi.astype(F32)).astype(BF16)
    sums = _dot(jnp.concatenate([hi, lo], axis=1), umat)
    carry = car_ref[...]
    reps = tk // carry.shape[1]
    carry_wide = carry if reps == 1 else jnp.concatenate([carry] * reps, axis=1)
    w = jnp.where(vis, jnp.exp(z + sums[:, :tk] + carry_wide), 0.0)
    acc_ref[...] += _dot(w.astype(BF16), v_blk)
    carry = carry + sums[:, tk:]
    car_ref[...] = carry
    return jnp.max(carry)


def _suffix_sum_matrix(tk):
    j = jnp.arange(2 * tk)[:, None] % tk
    s = jnp.arange(tk + V7X_LANES)[None, :]
    return jnp.where((s >= tk) | (j >= s), 1.0, 0.0).astype(BF16)


def _sb_prompt_kernel(q_ref, kt_ref, v_ref, u_ref, o_ref, acc_ref, car_ref, *, head_dim):
    qi = pl.program_id(2)
    q = q_ref[...]
    lane = lax.broadcasted_iota(jnp.int32, q.shape, 1)
    zero = jnp.zeros_like(q)
    qm = jnp.concatenate([jnp.where(lane < head_dim, q, zero), jnp.where(lane >= head_dim, q, zero)], axis=0)
    row_q = lax.broadcasted_iota(jnp.int32, (2 * TQ, TK), 0) & (TQ - 1)
    col = lax.broadcasted_iota(jnp.int32, (2 * TQ, TK), 1)
    dist = col - row_q
    acc_ref[...] = jnp.zeros_like(acc_ref)
    car_ref[...] = jnp.zeros_like(car_ref)
    umat = u_ref[...]

    def body(state):
        j, _ = state
        z = _dot(qm, kt_ref[0, 0, j])
        vis = dist < (qi - j) * TK
        return j - 1, _sb_block(z, vis, v_ref[0, j], umat, acc_ref, car_ref)

    lax.while_loop(lambda s: (s[0] >= 0) & (s[1] > NEG_CUT), body, (qi, jnp.float32(0.0)))
    acc = acc_ref[...]
    lane_o = lax.broadcasted_iota(jnp.int32, (TQ, acc.shape[1]), 1)
    o_ref[...] = jnp.where(lane_o < head_dim, acc[:TQ], acc[TQ:]).astype(o_ref.dtype)


def _sb_prompt(q, kt, vb, umat, *, batch, seq, head_dim):
    d = q.shape[1]
    n_hp = d // V7X_LANES
    nq = seq // TQ
    nk = seq // TK
    v4 = vb.reshape(batch, nk, TK, d)
    return pl.pallas_call(
        functools.partial(_sb_prompt_kernel, head_dim=head_dim),
        grid=(batch, n_hp, nq),
        in_specs=[pl.BlockSpec((TQ, V7X_LANES), lambda b, h, i: (b * nq + i, h)),
                  pl.BlockSpec((1, 1, nk, V7X_LANES, TK), lambda b, h, i: (b, h, 0, 0, 0)),
                  pl.BlockSpec((1, nk, TK, V7X_LANES), lambda b, h, i: (b, 0, 0, h)),
                  _const_spec(umat.shape)],
        out_specs=pl.BlockSpec((TQ, V7X_LANES), lambda b, h, i: (b * nq + i, h)),
        out_shape=jax.ShapeDtypeStruct((batch * seq, d), BF16),
        scratch_shapes=[pltpu.VMEM((2 * TQ, V7X_LANES), F32), pltpu.VMEM((2 * TQ, V7X_LANES), F32)],
        compiler_params=_params("arbitrary", "arbitrary", "arbitrary"),
        name="sb_prompt",
    )(q, kt, v4, umat)


def _sb_sample_kernel(q_ref, ck_ref, cv_ref, kn_ref, vn_ref, u_ref, o_ref,
                      acc_ref, car_ref, kpad_ref, vpad_ref, *, head_dim):
    t_new, d = q_ref.shape[1:]
    n_heads = d // head_dim
    rows = n_heads * t_new
    past = ck_ref.shape[1]
    q = q_ref[0]
    q_rep = jnp.concatenate([q] * n_heads, axis=0)
    row_head = lax.broadcasted_iota(jnp.int32, (rows, d), 0) // t_new
    lane_head = lax.broadcasted_iota(jnp.int32, (rows, d), 1) // head_dim
    qm = jnp.where(row_head == lane_head, q_rep, jnp.zeros_like(q_rep))
    row_q = lax.broadcasted_iota(jnp.int32, (rows, TK), 0) % t_new
    col = lax.broadcasted_iota(jnp.int32, (rows, TK), 1)
    dist = col - row_q
    acc_ref[...] = jnp.zeros_like(acc_ref)
    car_ref[...] = jnp.zeros_like(car_ref)
    umat = u_ref[...]

    def block(k_blk, v_blk, limit):
        z = lax.dot_general(qm, k_blk, (((1,), (1,)), ((), ())), preferred_element_type=F32)
        return _sb_block(z, dist < limit, v_blk, umat, acc_ref, car_ref)

    kpad_ref[...] = jnp.zeros_like(kpad_ref)
    vpad_ref[...] = jnp.zeros_like(vpad_ref)
    kpad_ref[0:t_new, :] = kn_ref[0].astype(BF16)
    vpad_ref[0:t_new, :] = vn_ref[0].astype(BF16)
    cmax = block(kpad_ref[...], vpad_ref[...], 0)

    def body(state):
        j, _ = state
        rows_j = pl.ds(pl.multiple_of(j * TK, TK), TK)
        return j - 1, block(ck_ref[0, rows_j, :].astype(BF16), cv_ref[0, rows_j, :].astype(BF16), ALL_VISIBLE)

    lax.while_loop(lambda s: (s[0] >= 0) & (s[1] > NEG_CUT), body, (past // TK - 1, cmax))
    acc = acc_ref[...]
    lane_head_o = lax.broadcasted_iota(jnp.int32, (t_new, d), 1) // head_dim
    o = jnp.zeros((t_new, d), F32)
    for h in range(n_heads):
        o = jnp.where(lane_head_o == h, acc[h * t_new:(h + 1) * t_new, :], o)
    o_ref[0] = o.astype(o_ref.dtype)


def _sb_sample(q, cache_k, cache_v, k_new, v_new, umat, *, head_dim):
    streams, t_new, d = q.shape
    past = cache_k.shape[1]
    rows = (d // head_dim) * t_new
    new_spec = pl.BlockSpec((1, t_new, d), lambda s: (s, 0, 0))
    cache_spec = pl.BlockSpec((1, past, d), lambda s: (s, 0, 0))
    return pl.pallas_call(
        functools.partial(_sb_sample_kernel, head_dim=head_dim),
        grid=(streams,),
        in_specs=[new_spec, cache_spec, cache_spec, new_spec, new_spec, _const_spec(umat.shape)],
        out_specs=new_spec,
        out_shape=jax.ShapeDtypeStruct((streams, t_new, d), BF16),
        scratch_shapes=[pltpu.VMEM((rows, d), F32), pltpu.VMEM((rows, V7X_LANES), F32),
                        pltpu.VMEM((TK, d), BF16), pltpu.VMEM((TK, d), BF16)],
        compiler_params=_params("arbitrary"),
        name="sb_sample",
    )(q, cache_k, cache_v, k_new, v_new, umat)


def _proj_ln_kernel(x_ref, op_ref, os_ref, wo_ref, g_ref, b_ref, o_ref, *, alpha, n_prompt_tiles):
    o = jnp.where(pl.program_id(0) < n_prompt_tiles, op_ref[...], os_ref[...])
    y = _dot(o, wo_ref[...])
    o_ref[...] = _layer_norm(alpha * x_ref[...] + y, g_ref[...], b_ref[...])


def _proj_ln(x, o_prompt, o_sample, wo, g, b, alpha):
    n, d = x.shape
    npt = o_prompt.shape[0] // TM
    assert o_sample.shape[0] == TM and n == (npt + 1) * TM
    return pl.pallas_call(
        functools.partial(_proj_ln_kernel, alpha=alpha, n_prompt_tiles=npt),
        grid=(n // TM,),
        in_specs=[pl.BlockSpec((TM, d), lambda i: (i, 0)),
                  pl.BlockSpec((TM, d), lambda i: (jnp.minimum(i, npt - 1), 0)),
                  pl.BlockSpec((TM, d), lambda i: (0, 0)),
                  _const_spec((d, d)), _const_spec((1, d)), _const_spec((1, d))],
        out_specs=pl.BlockSpec((TM, d), lambda i: (i, 0)),
        out_shape=jax.ShapeDtypeStruct((n, d), F32),
        compiler_params=_params("arbitrary"),
        name="proj_ln",
    )(x, o_prompt, o_sample, wo, g, b)


def kernel(x_prompt, x_sample, cache_k, cache_v, state_conv, ln_g, ln_b, w_ffn_gate, w_ffn_up, w_ffn_down,
           w_conv_in, w_conv, w_conv_out, w_kv, w_q, w_o):
    batch, seq, d = x_prompt.shape
    streams, t_new, _ = x_sample.shape
    past = cache_k.shape[1]
    depth = ln_g.shape[0]
    n_a = w_conv_in.shape[0]
    head_dim = d // N_HEADS
    alpha = (2.0 * depth) ** 0.25
    n_prompt = batch * seq
    n_sample = streams * t_new

    x = jnp.concatenate([x_prompt.reshape(n_prompt, d), x_sample.reshape(n_sample, d)], axis=0)
    zero_prev = jnp.zeros((batch, CONV_W - 1, d), F32)
    umat = _suffix_sum_matrix(TK)
    ln = lambda l, j: (ln_g[l, j][None, :], ln_b[l, j][None, :])

    conv_prompt, conv_sample = [], []
    for l in range(depth):
        x = _ffn_ln(x, w_ffn_gate[l, 0].astype(BF16), w_ffn_up[l, 0].astype(BF16),
                    w_ffn_down[l, 0].astype(BF16), *ln(l, 0), alpha)
        if l < n_a:
            conv_w = (w_conv_in[l].astype(BF16), w_conv[l], w_conv_out[l].astype(BF16), *ln(l, 1), alpha)
            x, st = _conv_ln(x, zero_prev, *conv_w, row0=0, streams=batch, t_len=seq, tm=TM_CONV)
            conv_prompt.append(st)
            x, st = _conv_ln(x, state_conv[l], *conv_w, row0=n_prompt, streams=streams, t_len=t_new, tm=t_new)
            conv_sample.append(st)
        else:
            i = l - n_a
            q = _q_proj(x, w_q[i].astype(BF16), head_dim ** -0.5)
            o_p = _sb_prompt(q, kt, vb, umat, batch=batch, seq=seq, head_dim=head_dim)
            o_s = _sb_sample(q[n_prompt:].reshape(streams, t_new, d),
                             cache_k.reshape(streams, past, d), cache_v.reshape(streams, past, d),
                             k_s.reshape(streams, t_new, d), v_s.reshape(streams, t_new, d),
                             umat, head_dim=head_dim)
            x = _proj_ln(x, o_p, o_s.reshape(n_sample, d), w_o[i].astype(BF16), *ln(l, 1), alpha)
        x = _ffn_ln(x, w_ffn_gate[l, 1].astype(BF16), w_ffn_up[l, 1].astype(BF16),
                    w_ffn_down[l, 1].astype(BF16), *ln(l, 2), alpha)
        if l == n_a - 1:
            k_p, v_p, k_s, v_s, kt, vb = _kv_proj(x, w_kv[:, :d].astype(BF16), w_kv[:, d:].astype(BF16),
                                                  n_prompt=n_prompt, batch=batch)

    heads = (N_HEADS, head_dim)
    return (x[:n_prompt].reshape(batch, seq, d),
            x[n_prompt:].reshape(streams, t_new, d),
            k_p.reshape(batch, seq, *heads), v_p.reshape(batch, seq, *heads),
            jnp.stack(conv_prompt, axis=0),
            k_s.reshape(streams, t_new, *heads), v_s.reshape(streams, t_new, *heads),
            jnp.stack(conv_sample, axis=0))
```

```python
import functools

import jax
import jax.numpy as jnp
from jax import lax
from jax.experimental import pallas as pl
from jax.experimental.pallas import tpu as pltpu

F32 = jnp.float32
BF16 = jnp.bfloat16

N_HEADS = 16
CONV_W = 3
LN_EPS = 1e-5

V7X_LANES = 128
VMEM_LIMIT_BYTES = 56 * 1024 * 1024

TM = 256
TQ = 256
TK = 256
TM_CONV = 512

NEG_CUT = -110.0
NO_BLOCK_BIAS = -1e30
SOFTPLUS_LINEAR = 80.0


def _dot(a, b):
    return jnp.dot(a, b, preferred_element_type=F32)


def _layer_norm(r, g, b):
    mu = jnp.mean(r, axis=-1, keepdims=True)
    xc = r - mu
    var = jnp.mean(xc * xc, axis=-1, keepdims=True)
    return xc * lax.rsqrt(var + LN_EPS) * g + b


def _params(*sem):
    return pltpu.CompilerParams(dimension_semantics=sem, vmem_limit_bytes=VMEM_LIMIT_BYTES)


def _const_spec(shape, index=None):
    if index is None:
        return pl.BlockSpec(shape, lambda *_: (0,) * len(shape), pipeline_mode=pl.Buffered(1))
    block = (None,) * len(index) + tuple(shape)
    full = tuple(index) + (0,) * len(shape)
    return pl.BlockSpec(block, lambda *_: full, pipeline_mode=pl.Buffered(1))


def _pair_specs(n_prompt, d):
    npt = n_prompt // TM
    prompt = pl.BlockSpec((TM, d), lambda i: (jnp.minimum(i, npt - 1), 0))
    sample = pl.BlockSpec((TM, d), lambda i: (0, 0))
    return npt, prompt, sample


def _pair_load(p_ref, s_ref, npt):
    return jnp.where(pl.program_id(0) < npt, p_ref[...], s_ref[...])


def _pair_store(p_ref, s_ref, npt, value):
    @pl.when(pl.program_id(0) < npt)
    def _():
        p_ref[...] = value.astype(p_ref.dtype)

    @pl.when(pl.program_id(0) >= npt)
    def _():
        s_ref[...] = value.astype(s_ref.dtype)


def _ffn_ln_kernel(xp_ref, xs_ref, wg_ref, wu_ref, wd_ref, g_ref, b_ref, op_ref, os_ref, *, alpha, ln_row, npt):
    x = _pair_load(xp_ref, xs_ref, npt)
    xb = x.astype(BF16)
    gate = _dot(xb, wg_ref[...])
    up = _dot(xb, wu_ref[...])
    h = (gate * jax.nn.sigmoid(gate) * up).astype(BF16)
    y = _dot(h, wd_ref[...])
    out = _layer_norm(alpha * x + 0.5 * y, g_ref[ln_row:ln_row + 1, :], b_ref[ln_row:ln_row + 1, :])
    _pair_store(op_ref, os_ref, npt, out)


def _ffn_ln(x_p, x_s, wg, wu, wd, ln_g, ln_b, *, layer, half, ln_row, alpha):
    n_prompt, d = x_p.shape
    assert x_s.shape == (TM, d)
    dff = wg.shape[-1]
    npt, prompt, sample = _pair_specs(n_prompt, d)
    return pl.pallas_call(
        functools.partial(_ffn_ln_kernel, alpha=alpha, ln_row=ln_row, npt=npt),
        grid=(npt + 1,),
        in_specs=[prompt, sample,
                  _const_spec((d, dff), (layer, half)), _const_spec((d, dff), (layer, half)),
                  _const_spec((dff, d), (layer, half)),
                  _const_spec(ln_g.shape), _const_spec(ln_b.shape)],
        out_specs=[prompt, sample],
        out_shape=[jax.ShapeDtypeStruct(x_p.shape, F32), jax.ShapeDtypeStruct(x_s.shape, F32)],
        compiler_params=_params("arbitrary"),
        name="ffn_ln",
    )(x_p, x_s, wg, wu, wd, ln_g, ln_b)


def _kv_kernel(xp_ref, xs_ref, wk_ref, wv_ref, kp_ref, vp_ref, ks_ref, vs_ref, kt_ref, vb_ref, *, npt):
    xb = _pair_load(xp_ref, xs_ref, npt).astype(BF16)
    k = _dot(xb, wk_ref[...])
    v = _dot(xb, wv_ref[...])
    _pair_store(kp_ref, ks_ref, npt, k)
    _pair_store(vp_ref, vs_ref, npt, v)

    @pl.when(pl.program_id(0) < npt)
    def _():
        vb_ref[...] = v.astype(BF16)
        for hp in range(kt_ref.shape[1]):
            kt_ref[0, hp, 0] = k[:, hp * V7X_LANES:(hp + 1) * V7X_LANES].T.astype(BF16)


def _kv_proj(x_p, x_s, w_kv, *, batch):
    n_prompt, d = x_p.shape
    assert TM == TK
    npt, prompt, sample = _pair_specs(n_prompt, d)
    tiles_per_stream = npt // batch
    n_hp = d // V7X_LANES
    kt_spec = pl.BlockSpec(
        (1, n_hp, 1, V7X_LANES, TK),
        lambda i: (jnp.minimum(i, npt - 1) // tiles_per_stream, 0, jnp.minimum(i, npt - 1) % tiles_per_stream, 0, 0))
    w_spec = lambda half: pl.BlockSpec((d, d), lambda i: (0, half), pipeline_mode=pl.Buffered(1))
    return pl.pallas_call(
        functools.partial(_kv_kernel, npt=npt),
        grid=(npt + 1,),
        in_specs=[prompt, sample, w_spec(0), w_spec(1)],
        out_specs=[prompt, prompt, sample, sample, kt_spec, prompt],
        out_shape=[jax.ShapeDtypeStruct(x_p.shape, F32), jax.ShapeDtypeStruct(x_p.shape, F32),
                   jax.ShapeDtypeStruct(x_s.shape, F32), jax.ShapeDtypeStruct(x_s.shape, F32),
                   jax.ShapeDtypeStruct((batch, n_hp, tiles_per_stream, V7X_LANES, TK), BF16),
                   jax.ShapeDtypeStruct(x_p.shape, BF16)],
        compiler_params=_params("arbitrary"),
        name="kv_proj",
    )(x_p, x_s, w_kv, w_kv)


def _q_kernel(xp_ref, xs_ref, wq_ref, qp_ref, qs_ref, *, scale, npt):
    q = _dot(_pair_load(xp_ref, xs_ref, npt).astype(BF16), wq_ref[...]) * scale
    _pair_store(qp_ref, qs_ref, npt, q)


def _q_proj(x_p, x_s, wq, *, index, scale):
    n_prompt, d = x_p.shape
    npt, prompt, sample = _pair_specs(n_prompt, d)
    return pl.pallas_call(
        functools.partial(_q_kernel, scale=scale, npt=npt),
        grid=(npt + 1,),
        in_specs=[prompt, sample, _const_spec((d, d), (index,))],
        out_specs=[prompt, sample],
        out_shape=[jax.ShapeDtypeStruct(x_p.shape, BF16), jax.ShapeDtypeStruct(x_s.shape, BF16)],
        compiler_params=_params("arbitrary"),
        name="q_proj",
    )(x_p, x_s, wq)


def _proj_ln_kernel(xp_ref, xs_ref, op_ref, os_ref, wo_ref, g_ref, b_ref, yp_ref, ys_ref, *, alpha, ln_row, npt):
    y = _dot(_pair_load(op_ref, os_ref, npt), wo_ref[...])
    x = _pair_load(xp_ref, xs_ref, npt)
    out = _layer_norm(alpha * x + y, g_ref[ln_row:ln_row + 1, :], b_ref[ln_row:ln_row + 1, :])
    _pair_store(yp_ref, ys_ref, npt, out)


def _proj_ln(x_p, x_s, o_p, o_s, wo, ln_g, ln_b, *, index, ln_row, alpha):
    n_prompt, d = x_p.shape
    npt, prompt, sample = _pair_specs(n_prompt, d)
    return pl.pallas_call(
        functools.partial(_proj_ln_kernel, alpha=alpha, ln_row=ln_row, npt=npt),
        grid=(npt + 1,),
        in_specs=[prompt, sample, prompt, sample, _const_spec((d, d), (index,)),
                  _const_spec(ln_g.shape), _const_spec(ln_b.shape)],
        out_specs=[prompt, sample],
        out_shape=[jax.ShapeDtypeStruct(x_p.shape, F32), jax.ShapeDtypeStruct(x_s.shape, F32)],
        compiler_params=_params("arbitrary"),
        name="proj_ln",
    )(x_p, x_s, o_p, o_s, wo, ln_g, ln_b)


def _conv_ln_kernel(x_ref, prev_ref, win_ref, wconv_ref, wout_ref, g_ref, b_ref,
                    o_ref, st_ref, car_ref, *, alpha, layer, ln_row):
    tm, d = x_ref.shape

    @pl.when(pl.program_id(1) == 0)
    def _():
        car_ref[...] = prev_ref[0]

    x = x_ref[...]
    p = _dot(x.astype(BF16), win_ref[...])
    gate_out = p[:, :d]
    u = p[:, d:2 * d] * p[:, 2 * d:]
    prev2 = car_ref[0:1, :]
    prev1 = car_ref[1:2, :]
    row = lax.broadcasted_iota(jnp.int32, (tm, d), 0)
    u1 = jnp.where(row == 0, prev1, pltpu.roll(u, 1, 0))
    u2 = jnp.where(row == 0, prev2, jnp.where(row == 1, prev1, pltpu.roll(u, 2, 0)))
    wc = wconv_ref[layer]
    conv = wc[0:1, :] * u2 + wc[1:2, :] * u1 + wc[2:3, :] * u
    y = _dot((gate_out * conv).astype(BF16), wout_ref[...])
    o_ref[...] = _layer_norm(alpha * x + y, g_ref[ln_row:ln_row + 1, :], b_ref[ln_row:ln_row + 1, :])
    last = u[tm - (CONV_W - 1):, :]
    car_ref[...] = last
    st_ref[0] = last


def _conv_ln(x, prev, win, wconv, wout, ln_g, ln_b, *, layer, ln_row, alpha, streams, tm):
    n, d = x.shape
    nt = n // streams // tm
    x_spec = pl.BlockSpec((tm, d), lambda s, t: (s * nt + t, 0))
    st_spec = pl.BlockSpec((1, CONV_W - 1, d), lambda s, t: (s, 0, 0))
    return pl.pallas_call(
        functools.partial(_conv_ln_kernel, alpha=alpha, layer=layer, ln_row=ln_row),
        grid=(streams, nt),
        in_specs=[x_spec, st_spec,
                  _const_spec((d, 3 * d), (layer,)), _const_spec(wconv.shape), _const_spec((d, d), (layer,)),
                  _const_spec(ln_g.shape), _const_spec(ln_b.shape)],
        out_specs=[x_spec, st_spec],
        out_shape=[jax.ShapeDtypeStruct((n, d), F32),
                   jax.ShapeDtypeStruct((streams, CONV_W - 1, d), F32)],
        scratch_shapes=[pltpu.VMEM((CONV_W - 1, d), F32)],
        compiler_params=_params("arbitrary", "arbitrary"),
        name="conv_ln",
    )(x, prev, win, wconv, wout, ln_g, ln_b)


def _sb_block(z, v_blk, umat, acc_ref, car_ref, vis=None, carry_bias=None):
    sub = V7X_LANES
    n_sub = z.shape[1] // sub
    softplus = jnp.maximum(z, jnp.log(1.0 + jnp.exp(jnp.minimum(z, SOFTPLUS_LINEAR))))
    if vis is not None:
        softplus = jnp.where(vis, softplus, 0.0)
    hi = softplus.astype(BF16)
    lo = (softplus - hi.astype(F32)).astype(BF16)
    run = car_ref[...]
    if carry_bias is not None:
        run = run + carry_bias
    args = [None] * n_sub
    for c in reversed(range(n_sub)):
        cols = slice(c * sub, (c + 1) * sub)
        sums = _dot(jnp.concatenate([hi[:, cols], lo[:, cols]], axis=1), umat)
        args[c] = z[:, cols] + sums[:, :sub] + run
        run = run + sums[:, sub:]
    w = jnp.exp(jnp.concatenate(args, axis=1))
    if vis is not None:
        w = jnp.where(vis, w, 0.0)
    acc_ref[...] += _dot(w.astype(BF16), v_blk)
    car_ref[...] = run
    return jnp.max(run)


def _suffix_sum_matrix():
    sub = V7X_LANES
    j = jnp.arange(2 * sub)[:, None] % sub
    s = jnp.arange(2 * sub)[None, :]
    return jnp.where((s >= sub) | (j >= s), -1.0, 0.0).astype(BF16)


def _sb_prompt_kernel(q_ref, kt_ref, v_ref, u_ref, o_ref, acc_ref, car_ref, *, head_dim):
    i = pl.program_id(2)
    tq = q_ref.shape[0]
    tk = kt_ref.shape[-1]
    q = q_ref[...]
    lane = lax.broadcasted_iota(jnp.int32, q.shape, 1)
    zero = jnp.zeros_like(q)
    qm = jnp.concatenate([jnp.where(lane < head_dim, q, zero), jnp.where(lane >= head_dim, q, zero)], axis=0)
    row_q = lax.broadcasted_iota(jnp.int32, (2 * tq, tk), 0) & (tq - 1)
    col = lax.broadcasted_iota(jnp.int32, (2 * tq, tk), 1)
    dist = col - row_q
    acc_ref[...] = jnp.zeros_like(acc_ref)
    car_ref[...] = jnp.zeros_like(car_ref)
    umat = u_ref[...]

    def block(j, vis=None, carry_bias=None):
        z = _dot(qm, kt_ref[0, 0, j])
        return _sb_block(z, v_ref[0, j], umat, acc_ref, car_ref, vis, carry_bias)

    block(i, vis=dist < 0)
    cmax = block(jnp.maximum(i - 1, 0), carry_bias=jnp.where(i > 0, 0.0, NO_BLOCK_BIAS))
    lax.while_loop(lambda s: (s[0] >= 0) & (s[1] > NEG_CUT),
                   lambda s: (s[0] - 1, block(s[0])), (i - 2, cmax))
    acc = acc_ref[...]
    lane_o = lax.broadcasted_iota(jnp.int32, (tq, acc.shape[1]), 1)
    o_ref[...] = jnp.where(lane_o < head_dim, acc[:tq], acc[tq:]).astype(o_ref.dtype)


def _sb_prompt(q, kt, vb, umat, *, batch, head_dim):
    n, d = q.shape
    assert TQ == TK
    seq = n // batch
    n_hp = d // V7X_LANES
    nq = seq // TQ
    nk = seq // TK
    v4 = vb.reshape(batch, nk, TK, d)
    return pl.pallas_call(
        functools.partial(_sb_prompt_kernel, head_dim=head_dim),
        grid=(batch, n_hp, nq),
        in_specs=[pl.BlockSpec((TQ, V7X_LANES), lambda b, h, i: (b * nq + i, h)),
                  pl.BlockSpec((1, 1, nk, V7X_LANES, TK), lambda b, h, i: (b, h, 0, 0, 0)),
                  pl.BlockSpec((1, nk, TK, V7X_LANES), lambda b, h, i: (b, 0, 0, h)),
                  _const_spec(umat.shape)],
        out_specs=pl.BlockSpec((TQ, V7X_LANES), lambda b, h, i: (b * nq + i, h)),
        out_shape=jax.ShapeDtypeStruct((n, d), BF16),
        scratch_shapes=[pltpu.VMEM((2 * TQ, V7X_LANES), F32), pltpu.VMEM((2 * TQ, V7X_LANES), F32)],
        compiler_params=_params("arbitrary", "arbitrary", "arbitrary"),
        name="sb_prompt",
    )(q, kt, v4, umat)


def _sb_sample_kernel(q_ref, ck_ref, cv_ref, kn_ref, vn_ref, u_ref, o_ref,
                      acc_ref, car_ref, kpad_ref, vpad_ref, *, head_dim):
    t_new, d = q_ref.shape[1:]
    n_heads = d // head_dim
    rows = n_heads * t_new
    past = ck_ref.shape[1]
    q = q_ref[0]
    q_rep = jnp.concatenate([q] * n_heads, axis=0)
    row_head = lax.broadcasted_iota(jnp.int32, (rows, d), 0) // t_new
    lane_head = lax.broadcasted_iota(jnp.int32, (rows, d), 1) // head_dim
    qm = jnp.where(row_head == lane_head, q_rep, jnp.zeros_like(q_rep))
    acc_ref[...] = jnp.zeros_like(acc_ref)
    car_ref[...] = jnp.zeros_like(car_ref)
    umat = u_ref[...]

    def block(k_blk, v_blk, vis):
        z = lax.dot_general(qm, k_blk, (((1,), (1,)), ((), ())), preferred_element_type=F32)
        return _sb_block(z, v_blk, umat, acc_ref, car_ref, vis)

    kpad_ref[...] = jnp.zeros_like(kpad_ref)
    vpad_ref[...] = jnp.zeros_like(vpad_ref)
    kpad_ref[0:t_new, :] = kn_ref[0].astype(BF16)
    vpad_ref[0:t_new, :] = vn_ref[0].astype(BF16)
    n_pad = kpad_ref.shape[0]
    row_q = lax.broadcasted_iota(jnp.int32, (rows, n_pad), 0) % t_new
    col = lax.broadcasted_iota(jnp.int32, (rows, n_pad), 1)
    cmax = block(kpad_ref[...], vpad_ref[...], col < row_q)

    def body(state):
        j, _ = state
        rows_j = pl.ds(pl.multiple_of(j * TK, TK), TK)
        return j - 1, block(ck_ref[0, rows_j, :].astype(BF16), cv_ref[0, rows_j, :].astype(BF16), None)

    lax.while_loop(lambda s: (s[0] >= 0) & (s[1] > NEG_CUT), body, (past // TK - 1, cmax))
    acc = acc_ref[...]
    lane_head_o = lax.broadcasted_iota(jnp.int32, (t_new, d), 1) // head_dim
    o = jnp.zeros((t_new, d), F32)
    for h in range(n_heads):
        o = jnp.where(lane_head_o == h, acc[h * t_new:(h + 1) * t_new, :], o)
    o_ref[0] = o.astype(o_ref.dtype)


def _sb_sample(q, cache_k, cache_v, k_new, v_new, umat, *, head_dim):
    streams, t_new, d = q.shape
    past = cache_k.shape[1]
    rows = (d // head_dim) * t_new
    new_spec = pl.BlockSpec((1, t_new, d), lambda s: (s, 0, 0))
    cache_spec = pl.BlockSpec((1, past, d), lambda s: (s, 0, 0))
    return pl.pallas_call(
        functools.partial(_sb_sample_kernel, head_dim=head_dim),
        grid=(streams,),
        in_specs=[new_spec, cache_spec, cache_spec, new_spec, new_spec, _const_spec(umat.shape)],
        out_specs=new_spec,
        out_shape=jax.ShapeDtypeStruct((streams, t_new, d), BF16),
        scratch_shapes=[pltpu.VMEM((rows, d), F32), pltpu.VMEM((rows, V7X_LANES), F32),
                        pltpu.VMEM((V7X_LANES, d), BF16), pltpu.VMEM((V7X_LANES, d), BF16)],
        compiler_params=_params("arbitrary"),
        name="sb_sample",
    )(q, cache_k, cache_v, k_new, v_new, umat)


def kernel(x_prompt, x_sample, cache_k, cache_v, state_conv, ln_g, ln_b, w_ffn_gate, w_ffn_up, w_ffn_down,
           w_conv_in, w_conv, w_conv_out, w_kv, w_q, w_o):
    batch, seq, d = x_prompt.shape
    streams, t_new, _ = x_sample.shape
    past = cache_k.shape[1]
    depth = ln_g.shape[0]
    n_a = w_conv_in.shape[0]
    head_dim = d // N_HEADS
    alpha = (2.0 * depth) ** 0.25

    x_p = x_prompt.reshape(batch * seq, d)
    x_s = x_sample.reshape(streams * t_new, d)
    zero_prev = jnp.zeros((batch, CONV_W - 1, d), F32)
    umat = _suffix_sum_matrix()
    ln_g2 = ln_g.reshape(depth * 3, d)
    ln_b2 = ln_b.reshape(depth * 3, d)
    wg, wu, wd = w_ffn_gate.astype(BF16), w_ffn_up.astype(BF16), w_ffn_down.astype(BF16)
    w_in, w_out = w_conv_in.astype(BF16), w_conv_out.astype(BF16)
    w_kv_b, w_q_b, w_o_b = w_kv.astype(BF16), w_q.astype(BF16), w_o.astype(BF16)
    ffn = functools.partial(_ffn_ln, wg=wg, wu=wu, wd=wd, ln_g=ln_g2, ln_b=ln_b2, alpha=alpha)

    conv_prompt, conv_sample = [], []
    for l in range(depth):
        x_p, x_s = ffn(x_p, x_s, layer=l, half=0, ln_row=3 * l)
        if l < n_a:
            conv = functools.partial(_conv_ln, win=w_in, wconv=w_conv, wout=w_out, ln_g=ln_g2, ln_b=ln_b2,
                                     layer=l, ln_row=3 * l + 1, alpha=alpha)
            x_p, st = conv(x_p, zero_prev, streams=batch, tm=TM_CONV)
            conv_prompt.append(st)
            x_s, st = conv(x_s, state_conv[l], streams=streams, tm=t_new)
            conv_sample.append(st)
        else:
            i = l - n_a
            q_p, q_s = _q_proj(x_p, x_s, w_q_b, index=i, scale=head_dim ** -0.5)
            o_p = _sb_prompt(q_p, kt, vb, umat, batch=batch, head_dim=head_dim)
            o_s = _sb_sample(q_s.reshape(streams, t_new, d),
                             cache_k.reshape(streams, past, d), cache_v.reshape(streams, past, d),
                             k_s.reshape(streams, t_new, d), v_s.reshape(streams, t_new, d),
                             umat, head_dim=head_dim)
            x_p, x_s = _proj_ln(x_p, x_s, o_p, o_s.reshape(streams * t_new, d), w_o_b, ln_g2, ln_b2,
                                index=i, ln_row=3 * l + 1, alpha=alpha)
        x_p, x_s = ffn(x_p, x_s, layer=l, half=1, ln_row=3 * l + 2)
        if l == n_a - 1:
            k_p, v_p, k_s, v_s, kt, vb = _kv_proj(x_p, x_s, w_kv_b, batch=batch)

    heads = (N_HEADS, head_dim)
    return (x_p.reshape(batch, seq, d),
            x_s.reshape(streams, t_new, d),
            k_p.reshape(batch, seq, *heads), v_p.reshape(batch, seq, *heads),
            jnp.stack(conv_prompt, axis=0),
            k_s.reshape(streams, t_new, *heads), v_s.reshape(streams, t_new, *heads),
            jnp.stack(conv_sample, axis=0))
```

```python
import functools

import jax
import jax.numpy as jnp
from jax import lax
from jax.experimental import pallas as pl
from jax.experimental.pallas import tpu as pltpu

F32 = jnp.float32
BF16 = jnp.bfloat16

N_HEADS = 16
CONV_W = 3
LN_EPS = 1e-5

V7X_LANES = 128
VMEM_LIMIT_BYTES = 56 * 1024 * 1024

TM = 256
TQ = 256
TK = 256
TM_CONV = 512

NEG_CUT = -110.0
NO_BLOCK_BIAS = -1e30
SOFTPLUS_LINEAR = 80.0


def _dot(a, b):
    return jnp.dot(a, b, preferred_element_type=F32)


def _layer_norm(r, g, b):
    mu = jnp.mean(r, axis=-1, keepdims=True)
    xc = r - mu
    var = jnp.mean(xc * xc, axis=-1, keepdims=True)
    return xc * lax.rsqrt(var + LN_EPS) * g + b


def _params(*sem):
    return pltpu.CompilerParams(dimension_semantics=sem, vmem_limit_bytes=VMEM_LIMIT_BYTES)


def _const_spec(shape, index=None):
    if index is None:
        return pl.BlockSpec(shape, lambda *_: (0,) * len(shape), pipeline_mode=pl.Buffered(1))
    block = (None,) * len(index) + tuple(shape)
    full = tuple(index) + (0,) * len(shape)
    return pl.BlockSpec(block, lambda *_: full, pipeline_mode=pl.Buffered(1))


def _pair_specs(n_prompt, d):
    npt = n_prompt // TM
    prompt = pl.BlockSpec((TM, d), lambda i: (jnp.minimum(i, npt - 1), 0))
    sample = pl.BlockSpec((TM, d), lambda i: (0, 0))
    return npt, prompt, sample


def _pair_load(p_ref, s_ref, npt):
    return jnp.where(pl.program_id(0) < npt, p_ref[...], s_ref[...])


def _pair_store(p_ref, s_ref, npt, value):
    @pl.when(pl.program_id(0) < npt)
    def _():
        p_ref[...] = value.astype(p_ref.dtype)

    @pl.when(pl.program_id(0) >= npt)
    def _():
        s_ref[...] = value.astype(s_ref.dtype)


def _ffn_ln_kernel(xp_ref, xs_ref, wg_ref, wu_ref, wd_ref, g_ref, b_ref, op_ref, os_ref, *, alpha, ln_row, npt):
    x = _pair_load(xp_ref, xs_ref, npt)
    xb = x.astype(BF16)
    gate = _dot(xb, wg_ref[...])
    up = _dot(xb, wu_ref[...])
    h = (gate * jax.nn.sigmoid(gate) * up).astype(BF16)
    y = _dot(h, wd_ref[...])
    out = _layer_norm(alpha * x + 0.5 * y, g_ref[ln_row:ln_row + 1, :], b_ref[ln_row:ln_row + 1, :])
    _pair_store(op_ref, os_ref, npt, out)


def _ffn_ln(x_p, x_s, wg, wu, wd, ln_g, ln_b, *, layer, half, ln_row, alpha):
    n_prompt, d = x_p.shape
    assert x_s.shape == (TM, d)
    dff = wg.shape[-1]
    npt, prompt, sample = _pair_specs(n_prompt, d)
    return pl.pallas_call(
        functools.partial(_ffn_ln_kernel, alpha=alpha, ln_row=ln_row, npt=npt),
        grid=(npt + 1,),
        in_specs=[prompt, sample,
                  _const_spec((d, dff), (layer, half)), _const_spec((d, dff), (layer, half)),
                  _const_spec((dff, d), (layer, half)),
                  _const_spec(ln_g.shape), _const_spec(ln_b.shape)],
        out_specs=[prompt, sample],
        out_shape=[jax.ShapeDtypeStruct(x_p.shape, F32), jax.ShapeDtypeStruct(x_s.shape, F32)],
        compiler_params=_params("arbitrary"),
        name="ffn_ln",
    )(x_p, x_s, wg, wu, wd, ln_g, ln_b)


def _store_heads(ref, value):
    head_dim = ref.shape[2]
    for h in range(ref.shape[1]):
        ref[:, h, :] = value[:, h * head_dim:(h + 1) * head_dim]


def _kv_kernel(xp_ref, xs_ref, wk_ref, wv_ref, kpt_ref, vpt_ref, ks_ref, vs_ref, ksf_ref, vsf_ref,
               kt_ref, vb_ref, *, npt):
    xb = _pair_load(xp_ref, xs_ref, npt).astype(BF16)
    k = _dot(xb, wk_ref[...])
    v = _dot(xb, wv_ref[...])

    @pl.when(pl.program_id(0) < npt)
    def _():
        vb_ref[...] = v.astype(BF16)
        for hp in range(kt_ref.shape[1]):
            lanes = slice(hp * V7X_LANES, (hp + 1) * V7X_LANES)
            k_t = k[:, lanes].T
            kpt_ref[0, lanes, :] = k_t
            kt_ref[0, hp, 0] = k_t.astype(BF16)
            vpt_ref[0, lanes, :] = v[:, lanes].T

    @pl.when(pl.program_id(0) >= npt)
    def _():
        _store_heads(ks_ref, k)
        _store_heads(vs_ref, v)
        ksf_ref[...] = k
        vsf_ref[...] = v


def _kv_proj(x_p, x_s, w_kv, *, batch, head_dim):
    n_prompt, d = x_p.shape
    assert TM == TK
    npt, prompt, sample = _pair_specs(n_prompt, d)
    tiles_per_stream = npt // batch
    n_hp = d // V7X_LANES
    heads = (d // head_dim, head_dim)
    tile_of = lambda i: jnp.minimum(i, npt - 1)
    prompt_t = pl.BlockSpec((1, d, TM), lambda i: (tile_of(i) // tiles_per_stream, 0, tile_of(i) % tiles_per_stream))
    sample4 = pl.BlockSpec((TM, *heads), lambda i: (0, 0, 0))
    kt_spec = pl.BlockSpec(
        (1, n_hp, 1, V7X_LANES, TK),
        lambda i: (tile_of(i) // tiles_per_stream, 0, tile_of(i) % tiles_per_stream, 0, 0))
    w_spec = lambda half: pl.BlockSpec((d, d), lambda i: (0, half), pipeline_mode=pl.Buffered(1))
    seq = n_prompt // batch
    return pl.pallas_call(
        functools.partial(_kv_kernel, npt=npt),
        grid=(npt + 1,),
        in_specs=[prompt, sample, w_spec(0), w_spec(1)],
        out_specs=[prompt_t, prompt_t, sample4, sample4, sample, sample, kt_spec, prompt],
        out_shape=[jax.ShapeDtypeStruct((batch, d, seq), F32), jax.ShapeDtypeStruct((batch, d, seq), F32),
                   jax.ShapeDtypeStruct((TM, *heads), F32), jax.ShapeDtypeStruct((TM, *heads), F32),
                   jax.ShapeDtypeStruct(x_s.shape, F32), jax.ShapeDtypeStruct(x_s.shape, F32),
                   jax.ShapeDtypeStruct((batch, n_hp, tiles_per_stream, V7X_LANES, TK), BF16),
                   jax.ShapeDtypeStruct(x_p.shape, BF16)],
        compiler_params=_params("arbitrary"),
        name="kv_proj",
    )(x_p, x_s, w_kv, w_kv)


def _q_kernel(xp_ref, xs_ref, wq_ref, qp_ref, qs_ref, *, scale, npt):
    q = _dot(_pair_load(xp_ref, xs_ref, npt).astype(BF16), wq_ref[...]) * scale
    _pair_store(qp_ref, qs_ref, npt, q)


def _q_proj(x_p, x_s, wq, *, index, scale):
    n_prompt, d = x_p.shape
    npt, prompt, sample = _pair_specs(n_prompt, d)
    return pl.pallas_call(
        functools.partial(_q_kernel, scale=scale, npt=npt),
        grid=(npt + 1,),
        in_specs=[prompt, sample, _const_spec((d, d), (index,))],
        out_specs=[prompt, sample],
        out_shape=[jax.ShapeDtypeStruct(x_p.shape, BF16), jax.ShapeDtypeStruct(x_s.shape, BF16)],
        compiler_params=_params("arbitrary"),
        name="q_proj",
    )(x_p, x_s, wq)


def _proj_ln_kernel(xp_ref, xs_ref, op_ref, os_ref, wo_ref, g_ref, b_ref, yp_ref, ys_ref, *, alpha, ln_row, npt):
    y = _dot(_pair_load(op_ref, os_ref, npt), wo_ref[...])
    x = _pair_load(xp_ref, xs_ref, npt)
    out = _layer_norm(alpha * x + y, g_ref[ln_row:ln_row + 1, :], b_ref[ln_row:ln_row + 1, :])
    _pair_store(yp_ref, ys_ref, npt, out)


def _proj_ln(x_p, x_s, o_p, o_s, wo, ln_g, ln_b, *, index, ln_row, alpha):
    n_prompt, d = x_p.shape
    npt, prompt, sample = _pair_specs(n_prompt, d)
    return pl.pallas_call(
        functools.partial(_proj_ln_kernel, alpha=alpha, ln_row=ln_row, npt=npt),
        grid=(npt + 1,),
        in_specs=[prompt, sample, prompt, sample, _const_spec((d, d), (index,)),
                  _const_spec(ln_g.shape), _const_spec(ln_b.shape)],
        out_specs=[prompt, sample],
        out_shape=[jax.ShapeDtypeStruct(x_p.shape, F32), jax.ShapeDtypeStruct(x_s.shape, F32)],
        compiler_params=_params("arbitrary"),
        name="proj_ln",
    )(x_p, x_s, o_p, o_s, wo, ln_g, ln_b)


def _conv_ln_kernel(x_ref, prev_ref, win_ref, wconv_ref, wout_ref, g_ref, b_ref,
                    o_ref, st_ref, car_ref, *, alpha, layer, ln_row):
    tm, d = x_ref.shape

    @pl.when(pl.program_id(1) == 0)
    def _():
        car_ref[...] = prev_ref[0]

    x = x_ref[...]
    p = _dot(x.astype(BF16), win_ref[...])
    gate_out = p[:, :d]
    u = p[:, d:2 * d] * p[:, 2 * d:]
    prev2 = car_ref[0:1, :]
    prev1 = car_ref[1:2, :]
    row = lax.broadcasted_iota(jnp.int32, (tm, d), 0)
    u1 = jnp.where(row == 0, prev1, pltpu.roll(u, 1, 0))
    u2 = jnp.where(row == 0, prev2, jnp.where(row == 1, prev1, pltpu.roll(u, 2, 0)))
    wc = wconv_ref[layer]
    conv = wc[0:1, :] * u2 + wc[1:2, :] * u1 + wc[2:3, :] * u
    y = _dot((gate_out * conv).astype(BF16), wout_ref[...])
    o_ref[...] = _layer_norm(alpha * x + y, g_ref[ln_row:ln_row + 1, :], b_ref[ln_row:ln_row + 1, :])
    last = u[tm - (CONV_W - 1):, :]
    car_ref[...] = last
    st_ref[0] = last


def _conv_ln(x, prev, win, wconv, wout, ln_g, ln_b, *, layer, ln_row, alpha, streams, tm):
    n, d = x.shape
    nt = n // streams // tm
    x_spec = pl.BlockSpec((tm, d), lambda s, t: (s * nt + t, 0))
    st_spec = pl.BlockSpec((1, CONV_W - 1, d), lambda s, t: (s, 0, 0))
    return pl.pallas_call(
        functools.partial(_conv_ln_kernel, alpha=alpha, layer=layer, ln_row=ln_row),
        grid=(streams, nt),
        in_specs=[x_spec, st_spec,
                  _const_spec((d, 3 * d), (layer,)), _const_spec(wconv.shape), _const_spec((d, d), (layer,)),
                  _const_spec(ln_g.shape), _const_spec(ln_b.shape)],
        out_specs=[x_spec, st_spec],
        out_shape=[jax.ShapeDtypeStruct((n, d), F32),
                   jax.ShapeDtypeStruct((streams, CONV_W - 1, d), F32)],
        scratch_shapes=[pltpu.VMEM((CONV_W - 1, d), F32)],
        compiler_params=_params("arbitrary", "arbitrary"),
        name="conv_ln",
    )(x, prev, win, wconv, wout, ln_g, ln_b)


def _sb_block(z, v_blk, umat, acc_ref, car_ref, vis=None, carry_bias=None, pv=_dot):
    sub = V7X_LANES
    n_sub = z.shape[1] // sub
    softplus = jnp.maximum(z, jnp.log(1.0 + jnp.exp(jnp.minimum(z, SOFTPLUS_LINEAR))))
    if vis is not None:
        softplus = jnp.where(vis, softplus, 0.0)
    hi = softplus.astype(BF16)
    lo = (softplus - hi.astype(F32)).astype(BF16)
    run = car_ref[...]
    if carry_bias is not None:
        run = run + carry_bias
    args = [None] * n_sub
    for c in reversed(range(n_sub)):
        cols = slice(c * sub, (c + 1) * sub)
        sums = _dot(jnp.concatenate([hi[:, cols], lo[:, cols]], axis=1), umat)
        args[c] = z[:, cols] + sums[:, :sub] + run
        run = run + sums[:, sub:]
    w = jnp.exp(jnp.concatenate(args, axis=1))
    if vis is not None:
        w = jnp.where(vis, w, 0.0)
    acc_ref[...] += pv(w.astype(BF16), v_blk)
    car_ref[...] = run
    return jnp.max(run)


def _suffix_sum_matrix():
    sub = V7X_LANES
    j = jnp.arange(2 * sub)[:, None] % sub
    s = jnp.arange(2 * sub)[None, :]
    return jnp.where((s >= sub) | (j >= s), -1.0, 0.0).astype(BF16)


def _sb_prompt_kernel(q_ref, kt_ref, v_ref, u_ref, o_ref, acc_ref, car_ref, *, head_dim):
    i = pl.program_id(2)
    tq = q_ref.shape[0]
    tk = kt_ref.shape[-1]
    q = q_ref[...]
    lane = lax.broadcasted_iota(jnp.int32, q.shape, 1)
    zero = jnp.zeros_like(q)
    qm = jnp.concatenate([jnp.where(lane < head_dim, q, zero), jnp.where(lane >= head_dim, q, zero)], axis=0)
    row_q = lax.broadcasted_iota(jnp.int32, (2 * tq, tk), 0) & (tq - 1)
    col = lax.broadcasted_iota(jnp.int32, (2 * tq, tk), 1)
    dist = col - row_q
    acc_ref[...] = jnp.zeros_like(acc_ref)
    car_ref[...] = jnp.zeros_like(car_ref)
    umat = u_ref[...]

    def block(j, vis=None, carry_bias=None):
        z = _dot(qm, kt_ref[0, 0, j])
        return _sb_block(z, v_ref[0, j], umat, acc_ref, car_ref, vis, carry_bias)

    block(i, vis=dist < 0)
    cmax = block(jnp.maximum(i - 1, 0), carry_bias=jnp.where(i > 0, 0.0, NO_BLOCK_BIAS))
    lax.while_loop(lambda s: (s[0] >= 0) & (s[1] > NEG_CUT),
                   lambda s: (s[0] - 1, block(s[0])), (i - 2, cmax))
    acc = acc_ref[...]
    lane_o = lax.broadcasted_iota(jnp.int32, (tq, acc.shape[1]), 1)
    o_ref[...] = jnp.where(lane_o < head_dim, acc[:tq], acc[tq:]).astype(o_ref.dtype)


def _sb_prompt(q, kt, vb, umat, *, batch, head_dim):
    n, d = q.shape
    assert TQ == TK
    seq = n // batch
    n_hp = d // V7X_LANES
    nq = seq // TQ
    nk = seq // TK
    v4 = vb.reshape(batch, nk, TK, d)
    return pl.pallas_call(
        functools.partial(_sb_prompt_kernel, head_dim=head_dim),
        grid=(batch, n_hp, nq),
        in_specs=[pl.BlockSpec((TQ, V7X_LANES), lambda b, h, i: (b * nq + i, h)),
                  pl.BlockSpec((1, 1, nk, V7X_LANES, TK), lambda b, h, i: (b, h, 0, 0, 0)),
                  pl.BlockSpec((1, nk, TK, V7X_LANES), lambda b, h, i: (b, 0, 0, h)),
                  _const_spec(umat.shape)],
        out_specs=pl.BlockSpec((TQ, V7X_LANES), lambda b, h, i: (b * nq + i, h)),
        out_shape=jax.ShapeDtypeStruct((n, d), BF16),
        scratch_shapes=[pltpu.VMEM((2 * TQ, V7X_LANES), F32), pltpu.VMEM((2 * TQ, V7X_LANES), F32)],
        compiler_params=_params("arbitrary", "arbitrary", "arbitrary"),
        name="sb_prompt",
    )(q, kt, v4, umat)


def _sb_sample_kernel(q_ref, kn_ref, vn_ref, u_ref, ck_hbm, cv_hbm, o_ref,
                      acc_ref, car_ref, kpad_ref, vpad_ref, kbuf, vbuf, sem, *, head_dim):
    s = pl.program_id(0)
    n_streams = pl.num_programs(0)
    t_new, d = q_ref.shape[1:]
    n_heads = d // head_dim
    rows = n_heads * t_new
    newest = ck_hbm.shape[2] // TK - 1
    slot = s % 2

    def copies(stream, j, into):
        keys = pl.ds(j * TK, TK)
        return (pltpu.make_async_copy(ck_hbm.at[stream, :, keys], kbuf.at[into], sem.at[0, into]),
                pltpu.make_async_copy(cv_hbm.at[stream, :, keys], vbuf.at[into], sem.at[1, into]))

    def start(stream, j, into):
        for c in copies(stream, j, into):
            c.start()

    def wait(stream, j, into):
        for c in copies(stream, j, into):
            c.wait()

    @pl.when(s == 0)
    def _():
        start(s, newest, slot)

    @pl.when(s + 1 < n_streams)
    def _():
        start(s + 1, newest, 1 - slot)

    q = q_ref[0]
    q_rep = jnp.concatenate([q] * n_heads, axis=0)
    row_head = lax.broadcasted_iota(jnp.int32, (rows, d), 0) // t_new
    lane_head = lax.broadcasted_iota(jnp.int32, (rows, d), 1) // head_dim
    qm = jnp.where(row_head == lane_head, q_rep, jnp.zeros_like(q_rep))
    acc_ref[...] = jnp.zeros_like(acc_ref)
    car_ref[...] = jnp.zeros_like(car_ref)
    umat = u_ref[...]

    contract_last = (((1,), (1,)), ((), ()))

    def cache_block():
        z = _dot(qm, kbuf[slot].astype(BF16))
        pv = lambda w, v_t: lax.dot_general(w, v_t, contract_last, preferred_element_type=F32)
        return _sb_block(z, vbuf[slot].astype(BF16), umat, acc_ref, car_ref, pv=pv)

    kpad_ref[...] = jnp.zeros_like(kpad_ref)
    vpad_ref[...] = jnp.zeros_like(vpad_ref)
    kpad_ref[0:t_new, :] = kn_ref[0].astype(BF16)
    vpad_ref[0:t_new, :] = vn_ref[0].astype(BF16)
    n_pad = kpad_ref.shape[0]
    row_q = lax.broadcasted_iota(jnp.int32, (rows, n_pad), 0) % t_new
    col = lax.broadcasted_iota(jnp.int32, (rows, n_pad), 1)
    z_new = lax.dot_general(qm, kpad_ref[...], contract_last, preferred_element_type=F32)
    _sb_block(z_new, vpad_ref[...], umat, acc_ref, car_ref, vis=col < row_q)

    wait(s, newest, slot)
    cmax = cache_block()

    def body(state):
        j, _ = state
        start(s, j, slot)
        wait(s, j, slot)
        return j - 1, cache_block()

    lax.while_loop(lambda st: (st[0] >= 0) & (st[1] > NEG_CUT), body, (newest - 1, cmax))
    acc = acc_ref[...]
    lane_head_o = lax.broadcasted_iota(jnp.int32, (t_new, d), 1) // head_dim
    o = jnp.zeros((t_new, d), F32)
    for h in range(n_heads):
        o = jnp.where(lane_head_o == h, acc[h * t_new:(h + 1) * t_new, :], o)
    o_ref[0] = o.astype(o_ref.dtype)


def _sb_sample(q, k_new, v_new, cache_kt, cache_vt, umat, *, head_dim):
    streams, t_new, d = q.shape
    n_heads = d // head_dim
    assert cache_kt.shape[1] == d and cache_kt.shape[2] % TK == 0
    rows = n_heads * t_new
    new_spec = pl.BlockSpec((1, t_new, d), lambda s: (s, 0, 0))
    hbm_spec = pl.BlockSpec(memory_space=pl.ANY)
    return pl.pallas_call(
        functools.partial(_sb_sample_kernel, head_dim=head_dim),
        grid=(streams,),
        in_specs=[new_spec, new_spec, new_spec, _const_spec(umat.shape), hbm_spec, hbm_spec],
        out_specs=new_spec,
        out_shape=jax.ShapeDtypeStruct((streams, t_new, d), BF16),
        scratch_shapes=[pltpu.VMEM((rows, d), F32), pltpu.VMEM((rows, V7X_LANES), F32),
                        pltpu.VMEM((V7X_LANES, d), BF16), pltpu.VMEM((V7X_LANES, d), BF16),
                        pltpu.VMEM((2, d, TK), F32), pltpu.VMEM((2, d, TK), F32),
                        pltpu.SemaphoreType.DMA((2, 2))],
        compiler_params=_params("arbitrary"),
        name="sb_sample",
    )(q, k_new, v_new, umat, cache_kt, cache_vt)


def kernel(x_prompt, x_sample, cache_k, cache_v, state_conv, ln_g, ln_b, w_ffn_gate, w_ffn_up, w_ffn_down,
           w_conv_in, w_conv, w_conv_out, w_kv, w_q, w_o):
    batch, seq, d = x_prompt.shape
    streams, t_new, _ = x_sample.shape
    depth = ln_g.shape[0]
    n_a = w_conv_in.shape[0]
    head_dim = d // N_HEADS
    alpha = (2.0 * depth) ** 0.25

    x_p = x_prompt.reshape(batch * seq, d)
    x_s = x_sample.reshape(streams * t_new, d)
    zero_prev = jnp.zeros((batch, CONV_W - 1, d), F32)
    past = cache_k.shape[1]
    cache_kt = jnp.transpose(cache_k, (0, 2, 3, 1)).reshape(streams, d, past)
    cache_vt = jnp.transpose(cache_v, (0, 2, 3, 1)).reshape(streams, d, past)
    umat = _suffix_sum_matrix()
    ln_g2 = ln_g.reshape(depth * 3, d)
    ln_b2 = ln_b.reshape(depth * 3, d)
    wg, wu, wd = w_ffn_gate.astype(BF16), w_ffn_up.astype(BF16), w_ffn_down.astype(BF16)
    w_in, w_out = w_conv_in.astype(BF16), w_conv_out.astype(BF16)
    w_kv_b, w_q_b, w_o_b = w_kv.astype(BF16), w_q.astype(BF16), w_o.astype(BF16)
    ffn = functools.partial(_ffn_ln, wg=wg, wu=wu, wd=wd, ln_g=ln_g2, ln_b=ln_b2, alpha=alpha)

    conv_prompt, conv_sample = [], []
    for l in range(depth):
        x_p, x_s = ffn(x_p, x_s, layer=l, half=0, ln_row=3 * l)
        if l < n_a:
            conv = functools.partial(_conv_ln, win=w_in, wconv=w_conv, wout=w_out, ln_g=ln_g2, ln_b=ln_b2,
                                     layer=l, ln_row=3 * l + 1, alpha=alpha)
            x_p, st = conv(x_p, zero_prev, streams=batch, tm=TM_CONV)
            conv_prompt.append(st)
            x_s, st = conv(x_s, state_conv[l], streams=streams, tm=t_new)
            conv_sample.append(st)
        else:
            i = l - n_a
            q_p, q_s = _q_proj(x_p, x_s, w_q_b, index=i, scale=head_dim ** -0.5)
            o_p = _sb_prompt(q_p, kt, vb, umat, batch=batch, head_dim=head_dim)
            o_s = _sb_sample(q_s.reshape(streams, t_new, d),
                             k_sf.reshape(streams, t_new, d), v_sf.reshape(streams, t_new, d),
                             cache_kt, cache_vt, umat, head_dim=head_dim)
            x_p, x_s = _proj_ln(x_p, x_s, o_p, o_s.reshape(streams * t_new, d), w_o_b, ln_g2, ln_b2,
                                index=i, ln_row=3 * l + 1, alpha=alpha)
        x_p, x_s = ffn(x_p, x_s, layer=l, half=1, ln_row=3 * l + 2)
        if l == n_a - 1:
            k_pt, v_pt, k_s, v_s, k_sf, v_sf, kt, vb = _kv_proj(x_p, x_s, w_kv_b, batch=batch, head_dim=head_dim)

    heads = (N_HEADS, head_dim)
    to_bthd = lambda a_t: jnp.transpose(a_t.reshape(batch, *heads, seq), (0, 3, 1, 2))
    return (x_p.reshape(batch, seq, d),
            x_s.reshape(streams, t_new, d),
            to_bthd(k_pt), to_bthd(v_pt),
            jnp.stack(conv_prompt, axis=0),
            k_s.reshape(streams, t_new, *heads), v_s.reshape(streams, t_new, *heads),
            jnp.stack(conv_sample, axis=0))
```

```python
import functools
from typing import NamedTuple

import jax
import jax.numpy as jnp
from jax import lax
from jax.experimental import pallas as pl
from jax.experimental.pallas import tpu as pltpu

F32 = jnp.float32
BF16 = jnp.bfloat16

N_HEADS = 16
CONV_W = 3
LN_EPS = 1e-5

V7X_LANES = 128
VMEM_LIMIT_BYTES = 56 * 1024 * 1024

TM = 256
TQ = 256
TK = 256
TM_CONV = 512

NEG_CUT = -110.0
NO_BLOCK_BIAS = -1e30
SOFTPLUS_LINEAR = 80.0


def _dot(a, b):
    return jnp.dot(a, b, preferred_element_type=F32)


def _layer_norm(r, g, b):
    mu = jnp.mean(r, axis=-1, keepdims=True)
    xc = r - mu
    var = jnp.mean(xc * xc, axis=-1, keepdims=True)
    return xc * lax.rsqrt(var + LN_EPS) * g + b


def _params(*sem):
    return pltpu.CompilerParams(dimension_semantics=sem, vmem_limit_bytes=VMEM_LIMIT_BYTES)


def _const_spec(shape, index=None):
    if index is None:
        return pl.BlockSpec(shape, lambda *_: (0,) * len(shape), pipeline_mode=pl.Buffered(1))
    block = (None,) * len(index) + tuple(shape)
    full = tuple(index) + (0,) * len(shape)
    return pl.BlockSpec(block, lambda *_: full, pipeline_mode=pl.Buffered(1))


def _pair_specs(n_prompt, d):
    npt = n_prompt // TM
    prompt = pl.BlockSpec((TM, d), lambda i: (jnp.minimum(i, npt - 1), 0))
    sample = pl.BlockSpec((TM, d), lambda i: (0, 0))
    return npt, prompt, sample


def _pair_load(p_ref, s_ref, npt):
    return jnp.where(pl.program_id(0) < npt, p_ref[...], s_ref[...])


def _pair_store(p_ref, s_ref, npt, value):
    @pl.when(pl.program_id(0) < npt)
    def _():
        p_ref[...] = value.astype(p_ref.dtype)

    @pl.when(pl.program_id(0) >= npt)
    def _():
        s_ref[...] = value.astype(s_ref.dtype)


def _ffn_ln_kernel(*refs, alpha, ln_row, npt, n_proj_w, n_proj_prompt, proj_prompt, proj_sample):
    xp_ref, xs_ref, wg_ref, wu_ref, wd_ref, g_ref, b_ref = refs[:7]
    proj_w = refs[7:7 + n_proj_w]
    op_ref, os_ref = refs[7 + n_proj_w:9 + n_proj_w]
    prompt_out = refs[9 + n_proj_w:9 + n_proj_w + n_proj_prompt]
    sample_out = refs[9 + n_proj_w + n_proj_prompt:-1]
    pre_ref = refs[-1]
    s = pl.program_id(0)
    last = npt + 1

    def finish(slot, x_ref, proj, out_refs):
        xn = _layer_norm(pre_ref[slot], g_ref[ln_row:ln_row + 1, :], b_ref[ln_row:ln_row + 1, :])
        x_ref[...] = xn
        if proj is not None:
            proj(xn, proj_w, out_refs)

    @pl.when(s == 0)
    def _():
        pre_ref[1] = jnp.zeros(pre_ref.shape[1:], F32)

    @pl.when(s < last)
    def _():
        finish((s + 1) % 2, op_ref, proj_prompt, prompt_out)
        x = jnp.where(s == 0, xs_ref[...], xp_ref[...])
        xb = x.astype(BF16)
        gate = _dot(xb, wg_ref[...])
        up = _dot(xb, wu_ref[...])
        h = (gate * jax.nn.sigmoid(gate) * up).astype(BF16)
        pre_ref[s % 2] = alpha * x + 0.5 * _dot(h, wd_ref[...])

    @pl.when(s == 1)
    def _():
        finish(0, os_ref, proj_sample, sample_out)

    @pl.when(s == last)
    def _():
        finish((last + 1) % 2, op_ref, proj_prompt, prompt_out)


def _ffn_ln(x_p, x_s, wg, wu, wd, ln_g, ln_b, *, layer, half, ln_row, alpha, proj=None):
    n_prompt, d = x_p.shape
    assert x_s.shape == (TM, d)
    dff = wg.shape[-1]
    npt = n_prompt // TM
    in_tile = lambda s: jnp.clip(s - 1, 0, npt - 1)
    out_tile = lambda s: jnp.clip(s - 2, 0, npt - 1)
    sample = pl.BlockSpec((TM, d), lambda s: (0, 0))
    prompt_out = pl.BlockSpec((TM, d), lambda s: (out_tile(s), 0))
    if proj is None:
        proj = _Projection((), (), (), (), (), (), None, None)
    else:
        proj = proj(n_prompt, d, out_tile)
    outs = pl.pallas_call(
        functools.partial(_ffn_ln_kernel, alpha=alpha, ln_row=ln_row, npt=npt, n_proj_w=len(proj.weights),
                          n_proj_prompt=len(proj.prompt_shapes),
                          proj_prompt=proj.prompt_fn, proj_sample=proj.sample_fn),
        grid=(npt + 2,),
        in_specs=[pl.BlockSpec((TM, d), lambda s: (in_tile(s), 0)), sample,
                  _const_spec((d, dff), (layer, half)), _const_spec((d, dff), (layer, half)),
                  _const_spec((dff, d), (layer, half)),
                  _const_spec(ln_g.shape), _const_spec(ln_b.shape), *proj.weight_specs],
        out_specs=[prompt_out, sample, *proj.prompt_specs, *proj.sample_specs],
        out_shape=[jax.ShapeDtypeStruct(x_p.shape, F32), jax.ShapeDtypeStruct(x_s.shape, F32),
                   *proj.prompt_shapes, *proj.sample_shapes],
        scratch_shapes=[pltpu.VMEM((2, TM, d), F32)],
        compiler_params=_params("arbitrary"),
        name="ffn_ln",
    )(x_p, x_s, wg, wu, wd, ln_g, ln_b, *proj.weights)
    return outs


class _Projection(NamedTuple):
    weights: tuple
    weight_specs: tuple
    prompt_shapes: tuple
    prompt_specs: tuple
    sample_shapes: tuple
    sample_specs: tuple
    prompt_fn: object
    sample_fn: object


def _q_projection(wq, *, index, scale):
    def store_q(xn, w_refs, out_refs):
        out_refs[0][...] = (_dot(xn.astype(BF16), w_refs[0][...]) * scale).astype(BF16)

    def build(n_prompt, d, out_tile):
        return _Projection(
            (wq,), (_const_spec((d, d), (index,)),),
            (jax.ShapeDtypeStruct((n_prompt, d), BF16),), (pl.BlockSpec((TM, d), lambda s: (out_tile(s), 0)),),
            (jax.ShapeDtypeStruct((TM, d), BF16),), (pl.BlockSpec((TM, d), lambda s: (0, 0)),),
            store_q, store_q)
    return build


def _store_heads(ref, value):
    head_dim = ref.shape[2]
    for h in range(ref.shape[1]):
        ref[:, h, :] = value[:, h * head_dim:(h + 1) * head_dim]


def _kv_projection(w_kv, *, batch, head_dim):
    def kv(xn, w_refs):
        xb = xn.astype(BF16)
        return _dot(xb, w_refs[0][...]), _dot(xb, w_refs[1][...])

    def store_prompt(xn, w_refs, out_refs):
        kpt_ref, vpt_ref, kt_ref, vb_ref = out_refs
        k, v = kv(xn, w_refs)
        vb_ref[...] = v.astype(BF16)
        for hp in range(kt_ref.shape[1]):
            lanes = slice(hp * V7X_LANES, (hp + 1) * V7X_LANES)
            k_t = k[:, lanes].T
            kpt_ref[0, lanes, :] = k_t
            kt_ref[0, hp, 0] = k_t.astype(BF16)
            vpt_ref[0, lanes, :] = v[:, lanes].T

    def store_sample(xn, w_refs, out_refs):
        ks_ref, vs_ref, ksf_ref, vsf_ref = out_refs
        k, v = kv(xn, w_refs)
        _store_heads(ks_ref, k)
        _store_heads(vs_ref, v)
        ksf_ref[...] = k
        vsf_ref[...] = v

    def build(n_prompt, d, out_tile):
        assert TM == TK
        seq = n_prompt // batch
        tiles_per_stream = seq // TM
        n_hp = d // V7X_LANES
        heads = (d // head_dim, head_dim)
        where = lambda s: (out_tile(s) // tiles_per_stream, out_tile(s) % tiles_per_stream)
        prompt_t = pl.BlockSpec((1, d, TM), lambda s: (where(s)[0], 0, where(s)[1]))
        kt_spec = pl.BlockSpec((1, n_hp, 1, V7X_LANES, TK), lambda s: (where(s)[0], 0, where(s)[1], 0, 0))
        sample4 = pl.BlockSpec((TM, *heads), lambda s: (0, 0, 0))
        sample2 = pl.BlockSpec((TM, d), lambda s: (0, 0))
        w_spec = lambda half: pl.BlockSpec((d, d), lambda s: (0, half), pipeline_mode=pl.Buffered(1))
        return _Projection(
            (w_kv, w_kv), (w_spec(0), w_spec(1)),
            (jax.ShapeDtypeStruct((batch, d, seq), F32), jax.ShapeDtypeStruct((batch, d, seq), F32),
             jax.ShapeDtypeStruct((batch, n_hp, tiles_per_stream, V7X_LANES, TK), BF16),
             jax.ShapeDtypeStruct((n_prompt, d), BF16)),
            (prompt_t, prompt_t, kt_spec, pl.BlockSpec((TM, d), lambda s: (out_tile(s), 0))),
            (jax.ShapeDtypeStruct((TM, *heads), F32), jax.ShapeDtypeStruct((TM, *heads), F32),
             jax.ShapeDtypeStruct((TM, d), F32), jax.ShapeDtypeStruct((TM, d), F32)),
            (sample4, sample4, sample2, sample2),
            store_prompt, store_sample)
    return build


def _proj_ln_kernel(xp_ref, xs_ref, op_ref, os_ref, wo_ref, g_ref, b_ref, yp_ref, ys_ref, *, alpha, ln_row, npt):
    y = _dot(_pair_load(op_ref, os_ref, npt), wo_ref[...])
    x = _pair_load(xp_ref, xs_ref, npt)
    out = _layer_norm(alpha * x + y, g_ref[ln_row:ln_row + 1, :], b_ref[ln_row:ln_row + 1, :])
    _pair_store(yp_ref, ys_ref, npt, out)


def _proj_ln(x_p, x_s, o_p, o_s, wo, ln_g, ln_b, *, index, ln_row, alpha):
    n_prompt, d = x_p.shape
    npt, prompt, sample = _pair_specs(n_prompt, d)
    return pl.pallas_call(
        functools.partial(_proj_ln_kernel, alpha=alpha, ln_row=ln_row, npt=npt),
        grid=(npt + 1,),
        in_specs=[prompt, sample, prompt, sample, _const_spec((d, d), (index,)),
                  _const_spec(ln_g.shape), _const_spec(ln_b.shape)],
        out_specs=[prompt, sample],
        out_shape=[jax.ShapeDtypeStruct(x_p.shape, F32), jax.ShapeDtypeStruct(x_s.shape, F32)],
        compiler_params=_params("arbitrary"),
        name="proj_ln",
    )(x_p, x_s, o_p, o_s, wo, ln_g, ln_b)


def _conv_ln_kernel(x_ref, prev_ref, win_ref, wconv_ref, wout_ref, g_ref, b_ref,
                    o_ref, st_ref, car_ref, *, alpha, layer, ln_row):
    tm, d = x_ref.shape

    @pl.when(pl.program_id(1) == 0)
    def _():
        car_ref[...] = prev_ref[0]

    x = x_ref[...]
    p = _dot(x.astype(BF16), win_ref[...])
    gate_out = p[:, :d]
    u = p[:, d:2 * d] * p[:, 2 * d:]
    prev2 = car_ref[0:1, :]
    prev1 = car_ref[1:2, :]
    row = lax.broadcasted_iota(jnp.int32, (tm, d), 0)
    u1 = jnp.where(row == 0, prev1, pltpu.roll(u, 1, 0))
    u2 = jnp.where(row == 0, prev2, jnp.where(row == 1, prev1, pltpu.roll(u, 2, 0)))
    wc = wconv_ref[layer]
    conv = wc[0:1, :] * u2 + wc[1:2, :] * u1 + wc[2:3, :] * u
    y = _dot((gate_out * conv).astype(BF16), wout_ref[...])
    o_ref[...] = _layer_norm(alpha * x + y, g_ref[ln_row:ln_row + 1, :], b_ref[ln_row:ln_row + 1, :])
    last = u[tm - (CONV_W - 1):, :]
    car_ref[...] = last
    st_ref[0] = last


def _conv_ln(x, prev, win, wconv, wout, ln_g, ln_b, *, layer, ln_row, alpha, streams, tm):
    n, d = x.shape
    nt = n // streams // tm
    x_spec = pl.BlockSpec((tm, d), lambda s, t: (s * nt + t, 0))
    st_spec = pl.BlockSpec((1, CONV_W - 1, d), lambda s, t: (s, 0, 0))
    return pl.pallas_call(
        functools.partial(_conv_ln_kernel, alpha=alpha, layer=layer, ln_row=ln_row),
        grid=(streams, nt),
        in_specs=[x_spec, st_spec,
                  _const_spec((d, 3 * d), (layer,)), _const_spec(wconv.shape), _const_spec((d, d), (layer,)),
                  _const_spec(ln_g.shape), _const_spec(ln_b.shape)],
        out_specs=[x_spec, st_spec],
        out_shape=[jax.ShapeDtypeStruct((n, d), F32),
                   jax.ShapeDtypeStruct((streams, CONV_W - 1, d), F32)],
        scratch_shapes=[pltpu.VMEM((CONV_W - 1, d), F32)],
        compiler_params=_params("arbitrary", "arbitrary"),
        name="conv_ln",
    )(x, prev, win, wconv, wout, ln_g, ln_b)


def _sb_block(z, v_blk, umat, acc_ref, car_ref, vis=None, carry_bias=None, pv=_dot):
    sub = V7X_LANES
    n_sub = z.shape[1] // sub
    softplus = jnp.maximum(z, jnp.log(1.0 + jnp.exp(jnp.minimum(z, SOFTPLUS_LINEAR))))
    if vis is not None:
        softplus = jnp.where(vis, softplus, 0.0)
    hi = softplus.astype(BF16)
    lo = (softplus - hi.astype(F32)).astype(BF16)
    run = car_ref[...]
    if carry_bias is not None:
        run = run + carry_bias
    args = [None] * n_sub
    for c in reversed(range(n_sub)):
        cols = slice(c * sub, (c + 1) * sub)
        sums = _dot(jnp.concatenate([hi[:, cols], lo[:, cols]], axis=1), umat)
        args[c] = z[:, cols] + sums[:, :sub] + run
        run = run + sums[:, sub:]
    w = jnp.exp(jnp.concatenate(args, axis=1))
    if vis is not None:
        w = jnp.where(vis, w, 0.0)
    acc_ref[...] += pv(w.astype(BF16), v_blk)
    car_ref[...] = run
    return jnp.max(run)


def _suffix_sum_matrix():
    sub = V7X_LANES
    j = jnp.arange(2 * sub)[:, None] % sub
    s = jnp.arange(2 * sub)[None, :]
    return jnp.where((s >= sub) | (j >= s), -1.0, 0.0).astype(BF16)


def _sb_prompt_kernel(q_ref, kt_ref, v_ref, u_ref, o_ref, acc_ref, car_ref, *, head_dim):
    i = pl.program_id(2)
    tq = q_ref.shape[0]
    tk = kt_ref.shape[-1]
    q = q_ref[...]
    lane = lax.broadcasted_iota(jnp.int32, q.shape, 1)
    zero = jnp.zeros_like(q)
    qm = jnp.concatenate([jnp.where(lane < head_dim, q, zero), jnp.where(lane >= head_dim, q, zero)], axis=0)
    row_q = lax.broadcasted_iota(jnp.int32, (2 * tq, tk), 0) & (tq - 1)
    col = lax.broadcasted_iota(jnp.int32, (2 * tq, tk), 1)
    dist = col - row_q
    acc_ref[...] = jnp.zeros_like(acc_ref)
    car_ref[...] = jnp.zeros_like(car_ref)
    umat = u_ref[...]

    def block(j, vis=None, carry_bias=None):
        z = _dot(qm, kt_ref[0, 0, j])
        return _sb_block(z, v_ref[0, j], umat, acc_ref, car_ref, vis, carry_bias)

    block(i, vis=dist < 0)
    cmax = block(jnp.maximum(i - 1, 0), carry_bias=jnp.where(i > 0, 0.0, NO_BLOCK_BIAS))
    lax.while_loop(lambda s: (s[0] >= 0) & (s[1] > NEG_CUT),
                   lambda s: (s[0] - 1, block(s[0])), (i - 2, cmax))
    acc = acc_ref[...]
    lane_o = lax.broadcasted_iota(jnp.int32, (tq, acc.shape[1]), 1)
    o_ref[...] = jnp.where(lane_o < head_dim, acc[:tq], acc[tq:]).astype(o_ref.dtype)


def _sb_prompt(q, kt, vb, umat, *, batch, head_dim):
    n, d = q.shape
    assert TQ == TK
    seq = n // batch
    n_hp = d // V7X_LANES
    nq = seq // TQ
    nk = seq // TK
    v4 = vb.reshape(batch, nk, TK, d)
    return pl.pallas_call(
        functools.partial(_sb_prompt_kernel, head_dim=head_dim),
        grid=(batch, n_hp, nq),
        in_specs=[pl.BlockSpec((TQ, V7X_LANES), lambda b, h, i: (b * nq + i, h)),
                  pl.BlockSpec((1, 1, nk, V7X_LANES, TK), lambda b, h, i: (b, h, 0, 0, 0)),
                  pl.BlockSpec((1, nk, TK, V7X_LANES), lambda b, h, i: (b, 0, 0, h)),
                  _const_spec(umat.shape)],
        out_specs=pl.BlockSpec((TQ, V7X_LANES), lambda b, h, i: (b * nq + i, h)),
        out_shape=jax.ShapeDtypeStruct((n, d), BF16),
        scratch_shapes=[pltpu.VMEM((2 * TQ, V7X_LANES), F32), pltpu.VMEM((2 * TQ, V7X_LANES), F32)],
        compiler_params=_params("arbitrary", "arbitrary", "arbitrary"),
        name="sb_prompt",
    )(q, kt, v4, umat)


def _sb_sample_kernel(q_ref, kn_ref, vn_ref, u_ref, ck_hbm, cv_hbm, o_ref,
                      acc_ref, car_ref, kpad_ref, vpad_ref, kbuf, vbuf, sem, *, head_dim):
    s = pl.program_id(0)
    n_streams = pl.num_programs(0)
    t_new, d = q_ref.shape[1:]
    n_heads = d // head_dim
    rows = n_heads * t_new
    newest = ck_hbm.shape[2] // TK - 1
    slot = s % 2

    def copies(stream, j, into):
        keys = pl.ds(j * TK, TK)
        return (pltpu.make_async_copy(ck_hbm.at[stream, :, keys], kbuf.at[into], sem.at[0, into]),
                pltpu.make_async_copy(cv_hbm.at[stream, :, keys], vbuf.at[into], sem.at[1, into]))

    def start(stream, j, into):
        for c in copies(stream, j, into):
            c.start()

    def wait(stream, j, into):
        for c in copies(stream, j, into):
            c.wait()

    @pl.when(s == 0)
    def _():
        start(s, newest, slot)

    @pl.when(s + 1 < n_streams)
    def _():
        start(s + 1, newest, 1 - slot)

    q = q_ref[0]
    q_rep = jnp.concatenate([q] * n_heads, axis=0)
    row_head = lax.broadcasted_iota(jnp.int32, (rows, d), 0) // t_new
    lane_head = lax.broadcasted_iota(jnp.int32, (rows, d), 1) // head_dim
    qm = jnp.where(row_head == lane_head, q_rep, jnp.zeros_like(q_rep))
    acc_ref[...] = jnp.zeros_like(acc_ref)
    car_ref[...] = jnp.zeros_like(car_ref)
    umat = u_ref[...]

    contract_last = (((1,), (1,)), ((), ()))

    def cache_block():
        z = _dot(qm, kbuf[slot].astype(BF16))
        pv = lambda w, v_t: lax.dot_general(w, v_t, contract_last, preferred_element_type=F32)
        return _sb_block(z, vbuf[slot].astype(BF16), umat, acc_ref, car_ref, pv=pv)

    kpad_ref[...] = jnp.zeros_like(kpad_ref)
    vpad_ref[...] = jnp.zeros_like(vpad_ref)
    kpad_ref[0:t_new, :] = kn_ref[0].astype(BF16)
    vpad_ref[0:t_new, :] = vn_ref[0].astype(BF16)
    n_pad = kpad_ref.shape[0]
    row_q = lax.broadcasted_iota(jnp.int32, (rows, n_pad), 0) % t_new
    col = lax.broadcasted_iota(jnp.int32, (rows, n_pad), 1)
    z_new = lax.dot_general(qm, kpad_ref[...], contract_last, preferred_element_type=F32)
    _sb_block(z_new, vpad_ref[...], umat, acc_ref, car_ref, vis=col < row_q)

    wait(s, newest, slot)
    cmax = cache_block()

    def body(state):
        j, _ = state
        start(s, j, slot)
        wait(s, j, slot)
        return j - 1, cache_block()

    lax.while_loop(lambda st: (st[0] >= 0) & (st[1] > NEG_CUT), body, (newest - 1, cmax))
    acc = acc_ref[...]
    lane_head_o = lax.broadcasted_iota(jnp.int32, (t_new, d), 1) // head_dim
    o = jnp.zeros((t_new, d), F32)
    for h in range(n_heads):
        o = jnp.where(lane_head_o == h, acc[h * t_new:(h + 1) * t_new, :], o)
    o_ref[0] = o.astype(o_ref.dtype)


def _sb_sample(q, k_new, v_new, cache_kt, cache_vt, umat, *, head_dim):
    streams, t_new, d = q.shape
    n_heads = d // head_dim
    assert cache_kt.shape[1] == d and cache_kt.shape[2] % TK == 0
    rows = n_heads * t_new
    new_spec = pl.BlockSpec((1, t_new, d), lambda s: (s, 0, 0))
    hbm_spec = pl.BlockSpec(memory_space=pl.ANY)
    return pl.pallas_call(
        functools.partial(_sb_sample_kernel, head_dim=head_dim),
        grid=(streams,),
        in_specs=[new_spec, new_spec, new_spec, _const_spec(umat.shape), hbm_spec, hbm_spec],
        out_specs=new_spec,
        out_shape=jax.ShapeDtypeStruct((streams, t_new, d), BF16),
        scratch_shapes=[pltpu.VMEM((rows, d), F32), pltpu.VMEM((rows, V7X_LANES), F32),
                        pltpu.VMEM((V7X_LANES, d), BF16), pltpu.VMEM((V7X_LANES, d), BF16),
                        pltpu.VMEM((2, d, TK), F32), pltpu.VMEM((2, d, TK), F32),
                        pltpu.SemaphoreType.DMA((2, 2))],
        compiler_params=_params("arbitrary"),
        name="sb_sample",
    )(q, k_new, v_new, umat, cache_kt, cache_vt)


def kernel(x_prompt, x_sample, cache_k, cache_v, state_conv, ln_g, ln_b, w_ffn_gate, w_ffn_up, w_ffn_down,
           w_conv_in, w_conv, w_conv_out, w_kv, w_q, w_o):
    batch, seq, d = x_prompt.shape
    streams, t_new, _ = x_sample.shape
    depth = ln_g.shape[0]
    n_a = w_conv_in.shape[0]
    head_dim = d // N_HEADS
    alpha = (2.0 * depth) ** 0.25

    x_p = x_prompt.reshape(batch * seq, d)
    x_s = x_sample.reshape(streams * t_new, d)
    zero_prev = jnp.zeros((batch, CONV_W - 1, d), F32)
    past = cache_k.shape[1]
    cache_kt = jnp.transpose(cache_k, (0, 2, 3, 1)).reshape(streams, d, past)
    cache_vt = jnp.transpose(cache_v, (0, 2, 3, 1)).reshape(streams, d, past)
    umat = _suffix_sum_matrix()
    ln_g2 = ln_g.reshape(depth * 3, d)
    ln_b2 = ln_b.reshape(depth * 3, d)
    wg, wu, wd = w_ffn_gate.astype(BF16), w_ffn_up.astype(BF16), w_ffn_down.astype(BF16)
    w_in, w_out = w_conv_in.astype(BF16), w_conv_out.astype(BF16)
    w_kv_b, w_q_b, w_o_b = w_kv.astype(BF16), w_q.astype(BF16), w_o.astype(BF16)
    ffn = functools.partial(_ffn_ln, wg=wg, wu=wu, wd=wd, ln_g=ln_g2, ln_b=ln_b2, alpha=alpha)

    conv_prompt, conv_sample = [], []
    for l in range(depth):
        if l < n_a:
            x_p, x_s = ffn(x_p, x_s, layer=l, half=0, ln_row=3 * l)
            conv = functools.partial(_conv_ln, win=w_in, wconv=w_conv, wout=w_out, ln_g=ln_g2, ln_b=ln_b2,
                                     layer=l, ln_row=3 * l + 1, alpha=alpha)
            x_p, st = conv(x_p, zero_prev, streams=batch, tm=TM_CONV)
            conv_prompt.append(st)
            x_s, st = conv(x_s, state_conv[l], streams=streams, tm=t_new)
            conv_sample.append(st)
        else:
            i = l - n_a
            x_p, x_s, q_p, q_s = ffn(x_p, x_s, layer=l, half=0, ln_row=3 * l,
                                     proj=_q_projection(w_q_b, index=i, scale=head_dim ** -0.5))
            o_p = _sb_prompt(q_p, kt, vb, umat, batch=batch, head_dim=head_dim)
            o_s = _sb_sample(q_s.reshape(streams, t_new, d),
                             k_sf.reshape(streams, t_new, d), v_sf.reshape(streams, t_new, d),
                             cache_kt, cache_vt, umat, head_dim=head_dim)
            x_p, x_s = _proj_ln(x_p, x_s, o_p, o_s.reshape(streams * t_new, d), w_o_b, ln_g2, ln_b2,
                                index=i, ln_row=3 * l + 1, alpha=alpha)
        if l == n_a - 1:
            x_p, x_s, k_pt, v_pt, kt, vb, k_s, v_s, k_sf, v_sf = ffn(
                x_p, x_s, layer=l, half=1, ln_row=3 * l + 2,
                proj=_kv_projection(w_kv_b, batch=batch, head_dim=head_dim))
        else:
            x_p, x_s = ffn(x_p, x_s, layer=l, half=1, ln_row=3 * l + 2)

    heads = (N_HEADS, head_dim)
    to_bthd = lambda a_t: jnp.transpose(a_t.reshape(batch, *heads, seq), (0, 3, 1, 2))
    return (x_p.reshape(batch, seq, d),
            x_s.reshape(streams, t_new, d),
            to_bthd(k_pt), to_bthd(v_pt),
            jnp.stack(conv_prompt, axis=0),
            k_s.reshape(streams, t_new, *heads), v_s.reshape(streams, t_new, *heads),
            jnp.stack(conv_sample, axis=0))
```

```python
import functools
from typing import NamedTuple

import jax
import jax.numpy as jnp
from jax import lax
from jax.experimental import pallas as pl
from jax.experimental.pallas import tpu as pltpu

F32 = jnp.float32
BF16 = jnp.bfloat16

N_HEADS = 16
CONV_W = 3
LN_EPS = 1e-5

V7X_LANES = 128
VMEM_LIMIT_BYTES = 56 * 1024 * 1024

TM = 256
TM_FFN = 256
TM_FFN_KV = 256
FFN_CHUNKS = 1
HP_PER_STEP = 4
TQ = 256
TK = 256
TM_CONV = 512

NEG_CUT = -110.0
NO_BLOCK_BIAS = -1e30
SOFTPLUS_LINEAR = 80.0


def _dot(a, b):
    return jnp.dot(a, b, preferred_element_type=F32)


def _layer_norm(r, g, b):
    mu = jnp.mean(r, axis=-1, keepdims=True)
    xc = r - mu
    var = jnp.mean(xc * xc, axis=-1, keepdims=True)
    return xc * lax.rsqrt(var + LN_EPS) * g + b


def _params(*sem):
    return pltpu.CompilerParams(dimension_semantics=sem, vmem_limit_bytes=VMEM_LIMIT_BYTES)


def _const_spec(shape, index=None):
    if index is None:
        return pl.BlockSpec(shape, lambda *_: (0,) * len(shape), pipeline_mode=pl.Buffered(1))
    block = (None,) * len(index) + tuple(shape)
    full = tuple(index) + (0,) * len(shape)
    return pl.BlockSpec(block, lambda *_: full, pipeline_mode=pl.Buffered(1))


def _pair_specs(n_prompt, d):
    npt = n_prompt // TM
    prompt = pl.BlockSpec((TM, d), lambda i: (jnp.minimum(i, npt - 1), 0))
    sample = pl.BlockSpec((TM, d), lambda i: (0, 0))
    return npt, prompt, sample


def _pair_load(p_ref, s_ref, npt):
    return jnp.where(pl.program_id(0) < npt, p_ref[...], s_ref[...])


def _pair_store(p_ref, s_ref, npt, value):
    @pl.when(pl.program_id(0) < npt)
    def _():
        p_ref[...] = value.astype(p_ref.dtype)

    @pl.when(pl.program_id(0) >= npt)
    def _():
        s_ref[...] = value.astype(s_ref.dtype)


def _ffn_ln_kernel(*refs, alpha, ln_row, npt, ff_chunks, n_proj_w, n_proj_prompt, proj_prompt, proj_sample):
    xp_ref, xs_ref, wg_ref, wu_ref, wd_ref, g_ref, b_ref = refs[:7]
    proj_w = refs[7:7 + n_proj_w]
    op_ref, os_ref = refs[7 + n_proj_w:9 + n_proj_w]
    prompt_out = refs[9 + n_proj_w:9 + n_proj_w + n_proj_prompt]
    sample_out = refs[9 + n_proj_w + n_proj_prompt:-2]
    pre_ref, pre_s_ref = refs[-2:]
    s = pl.program_id(0)
    ff_step = wg_ref.shape[1] // ff_chunks

    def pre_norm(x):
        xb = x.astype(BF16)
        y = None
        for c in range(ff_chunks):
            cols = slice(c * ff_step, (c + 1) * ff_step)
            gate = _dot(xb, wg_ref[:, cols])
            up = _dot(xb, wu_ref[:, cols])
            part = _dot((gate * jax.nn.sigmoid(gate) * up).astype(BF16), wd_ref[cols, :])
            y = part if y is None else y + part
        return alpha * x + 0.5 * y

    def finish(pre, x_ref, proj, out_refs):
        xn = _layer_norm(pre, g_ref[ln_row:ln_row + 1, :], b_ref[ln_row:ln_row + 1, :])
        x_ref[...] = xn
        if proj is not None:
            proj(xn, proj_w, out_refs)

    @pl.when(s == 0)
    def _():
        pre_ref[1] = jnp.zeros(pre_ref.shape[1:], F32)
        pre_s_ref[...] = pre_norm(xs_ref[...])

    @pl.when((s >= 1) & (s <= npt))
    def _():
        finish(pre_ref[s % 2], op_ref, proj_prompt, prompt_out)
        pre_ref[(s + 1) % 2] = pre_norm(xp_ref[...])

    @pl.when(s == 1)
    def _():
        finish(pre_s_ref[...], os_ref, proj_sample, sample_out)

    @pl.when(s == npt + 1)
    def _():
        finish(pre_ref[(npt + 1) % 2], op_ref, proj_prompt, prompt_out)


def _ffn_ln(x_p, x_s, wg, wu, wd, ln_g, ln_b, *, layer, half, ln_row, alpha, tm, ff_chunks, proj=None):
    n_prompt, d = x_p.shape
    ts = x_s.shape[0]
    dff = wg.shape[-1]
    npt = n_prompt // tm
    in_tile = lambda s: jnp.clip(s - 1, 0, npt - 1)
    out_tile = lambda s: jnp.clip(s - 2, 0, npt - 1)
    sample = pl.BlockSpec((ts, d), lambda s: (0, 0))
    prompt_out = pl.BlockSpec((tm, d), lambda s: (out_tile(s), 0))
    if proj is None:
        proj = _Projection((), (), (), (), (), (), None, None)
    else:
        proj = proj(n_prompt, ts, d, tm, out_tile)
    outs = pl.pallas_call(
        functools.partial(_ffn_ln_kernel, alpha=alpha, ln_row=ln_row, npt=npt, ff_chunks=ff_chunks,
                          n_proj_w=len(proj.weights), n_proj_prompt=len(proj.prompt_shapes),
                          proj_prompt=proj.prompt_fn, proj_sample=proj.sample_fn),
        grid=(npt + 2,),
        in_specs=[pl.BlockSpec((tm, d), lambda s: (in_tile(s), 0)), sample,
                  _const_spec((d, dff), (layer, half)), _const_spec((d, dff), (layer, half)),
                  _const_spec((dff, d), (layer, half)),
                  _const_spec(ln_g.shape), _const_spec(ln_b.shape), *proj.weight_specs],
        out_specs=[prompt_out, sample, *proj.prompt_specs, *proj.sample_specs],
        out_shape=[jax.ShapeDtypeStruct(x_p.shape, F32), jax.ShapeDtypeStruct(x_s.shape, F32),
                   *proj.prompt_shapes, *proj.sample_shapes],
        scratch_shapes=[pltpu.VMEM((2, tm, d), F32), pltpu.VMEM((ts, d), F32)],
        compiler_params=_params("arbitrary"),
        name="ffn_ln",
    )(x_p, x_s, wg, wu, wd, ln_g, ln_b, *proj.weights)
    return outs


class _Projection(NamedTuple):
    weights: tuple
    weight_specs: tuple
    prompt_shapes: tuple
    prompt_specs: tuple
    sample_shapes: tuple
    sample_specs: tuple
    prompt_fn: object
    sample_fn: object


def _q_projection(wq, *, index, scale):
    def store_q(xn, w_refs, out_refs):
        out_refs[0][...] = (_dot(xn.astype(BF16), w_refs[0][...]) * scale).astype(BF16)

    def build(n_prompt, n_sample, d, tm, out_tile):
        return _Projection(
            (wq,), (_const_spec((d, d), (index,)),),
            (jax.ShapeDtypeStruct((n_prompt, d), BF16),), (pl.BlockSpec((tm, d), lambda s: (out_tile(s), 0)),),
            (jax.ShapeDtypeStruct((n_sample, d), BF16),), (pl.BlockSpec((n_sample, d), lambda s: (0, 0)),),
            store_q, store_q)
    return build


def _store_heads(ref, value):
    head_dim = ref.shape[2]
    for h in range(ref.shape[1]):
        ref[:, h, :] = value[:, h * head_dim:(h + 1) * head_dim]


def _kv_projection(w_kv, *, batch, head_dim):
    def kv(xn, w_refs):
        xb = xn.astype(BF16)
        return _dot(xb, w_refs[0][...]), _dot(xb, w_refs[1][...])

    def store_prompt(xn, w_refs, out_refs):
        kpt_ref, vpt_ref, kt_ref, vb_ref = out_refs
        k, v = kv(xn, w_refs)
        vb_ref[...] = v.astype(BF16)
        for hp in range(kt_ref.shape[1]):
            lanes = slice(hp * V7X_LANES, (hp + 1) * V7X_LANES)
            k_t = k[:, lanes].T
            kpt_ref[0, lanes, :] = k_t
            for c in range(kt_ref.shape[2]):
                kt_ref[0, hp, c] = k_t[:, c * TK:(c + 1) * TK].astype(BF16)
            vpt_ref[0, lanes, :] = v[:, lanes].T

    def store_sample(xn, w_refs, out_refs):
        ks_ref, vs_ref, ksf_ref, vsf_ref = out_refs
        k, v = kv(xn, w_refs)
        _store_heads(ks_ref, k)
        _store_heads(vs_ref, v)
        ksf_ref[...] = k
        vsf_ref[...] = v

    def build(n_prompt, n_sample, d, tm, out_tile):
        seq = n_prompt // batch
        tiles_per_stream = seq // tm
        blocks_per_tile = tm // TK
        n_hp = d // V7X_LANES
        heads = (d // head_dim, head_dim)
        where = lambda s: (out_tile(s) // tiles_per_stream, out_tile(s) % tiles_per_stream)
        prompt_t = pl.BlockSpec((1, d, tm), lambda s: (where(s)[0], 0, where(s)[1]))
        kt_spec = pl.BlockSpec((1, n_hp, blocks_per_tile, V7X_LANES, TK),
                               lambda s: (where(s)[0], 0, where(s)[1], 0, 0))
        sample4 = pl.BlockSpec((n_sample, *heads), lambda s: (0, 0, 0))
        sample2 = pl.BlockSpec((n_sample, d), lambda s: (0, 0))
        w_spec = lambda half: pl.BlockSpec((d, d), lambda s: (0, half), pipeline_mode=pl.Buffered(1))
        return _Projection(
            (w_kv, w_kv), (w_spec(0), w_spec(1)),
            (jax.ShapeDtypeStruct((batch, d, seq), F32), jax.ShapeDtypeStruct((batch, d, seq), F32),
             jax.ShapeDtypeStruct((batch, n_hp, seq // TK, V7X_LANES, TK), BF16),
             jax.ShapeDtypeStruct((n_prompt, d), BF16)),
            (prompt_t, prompt_t, kt_spec, pl.BlockSpec((tm, d), lambda s: (out_tile(s), 0))),
            (jax.ShapeDtypeStruct((n_sample, *heads), F32), jax.ShapeDtypeStruct((n_sample, *heads), F32),
             jax.ShapeDtypeStruct((n_sample, d), F32), jax.ShapeDtypeStruct((n_sample, d), F32)),
            (sample4, sample4, sample2, sample2),
            store_prompt, store_sample)
    return build


def _proj_ln_kernel(xp_ref, xs_ref, op_ref, os_ref, wo_ref, g_ref, b_ref, yp_ref, ys_ref, *, alpha, ln_row, npt):
    y = _dot(_pair_load(op_ref, os_ref, npt), wo_ref[...])
    x = _pair_load(xp_ref, xs_ref, npt)
    out = _layer_norm(alpha * x + y, g_ref[ln_row:ln_row + 1, :], b_ref[ln_row:ln_row + 1, :])
    _pair_store(yp_ref, ys_ref, npt, out)


def _proj_ln(x_p, x_s, o_p, o_s, wo, ln_g, ln_b, *, index, ln_row, alpha):
    n_prompt, d = x_p.shape
    npt, prompt, sample = _pair_specs(n_prompt, d)
    return pl.pallas_call(
        functools.partial(_proj_ln_kernel, alpha=alpha, ln_row=ln_row, npt=npt),
        grid=(npt + 1,),
        in_specs=[prompt, sample, prompt, sample, _const_spec((d, d), (index,)),
                  _const_spec(ln_g.shape), _const_spec(ln_b.shape)],
        out_specs=[prompt, sample],
        out_shape=[jax.ShapeDtypeStruct(x_p.shape, F32), jax.ShapeDtypeStruct(x_s.shape, F32)],
        compiler_params=_params("arbitrary"),
        name="proj_ln",
    )(x_p, x_s, o_p, o_s, wo, ln_g, ln_b)


def _conv_ln_kernel(x_ref, prev_ref, win_ref, wconv_ref, wout_ref, g_ref, b_ref,
                    o_ref, st_ref, car_ref, *, alpha, layer, ln_row):
    tm, d = x_ref.shape

    @pl.when(pl.program_id(1) == 0)
    def _():
        car_ref[...] = prev_ref[0]

    x = x_ref[...]
    p = _dot(x.astype(BF16), win_ref[...])
    gate_out = p[:, :d]
    u = p[:, d:2 * d] * p[:, 2 * d:]
    prev2 = car_ref[0:1, :]
    prev1 = car_ref[1:2, :]
    row = lax.broadcasted_iota(jnp.int32, (tm, d), 0)
    u1 = jnp.where(row == 0, prev1, pltpu.roll(u, 1, 0))
    u2 = jnp.where(row == 0, prev2, jnp.where(row == 1, prev1, pltpu.roll(u, 2, 0)))
    wc = wconv_ref[layer]
    conv = wc[0:1, :] * u2 + wc[1:2, :] * u1 + wc[2:3, :] * u
    y = _dot((gate_out * conv).astype(BF16), wout_ref[...])
    o_ref[...] = _layer_norm(alpha * x + y, g_ref[ln_row:ln_row + 1, :], b_ref[ln_row:ln_row + 1, :])
    last = u[tm - (CONV_W - 1):, :]
    car_ref[...] = last
    st_ref[0] = last


def _conv_ln(x, prev, win, wconv, wout, ln_g, ln_b, *, layer, ln_row, alpha, streams, tm):
    n, d = x.shape
    nt = n // streams // tm
    x_spec = pl.BlockSpec((tm, d), lambda s, t: (s * nt + t, 0))
    st_spec = pl.BlockSpec((1, CONV_W - 1, d), lambda s, t: (s, 0, 0))
    return pl.pallas_call(
        functools.partial(_conv_ln_kernel, alpha=alpha, layer=layer, ln_row=ln_row),
        grid=(streams, nt),
        in_specs=[x_spec, st_spec,
                  _const_spec((d, 3 * d), (layer,)), _const_spec(wconv.shape), _const_spec((d, d), (layer,)),
                  _const_spec(ln_g.shape), _const_spec(ln_b.shape)],
        out_specs=[x_spec, st_spec],
        out_shape=[jax.ShapeDtypeStruct((n, d), F32),
                   jax.ShapeDtypeStruct((streams, CONV_W - 1, d), F32)],
        scratch_shapes=[pltpu.VMEM((CONV_W - 1, d), F32)],
        compiler_params=_params("arbitrary", "arbitrary"),
        name="conv_ln",
    )(x, prev, win, wconv, wout, ln_g, ln_b)


def _sb_block(z, v_blk, umat, acc_ref, car_ref, vis=None, carry_bias=None, pv=_dot):
    sub = V7X_LANES
    n_sub = z.shape[1] // sub
    softplus = jnp.maximum(z, jnp.log(1.0 + jnp.exp(jnp.minimum(z, SOFTPLUS_LINEAR))))
    if vis is not None:
        softplus = jnp.where(vis, softplus, 0.0)
    hi = softplus.astype(BF16)
    lo = (softplus - hi.astype(F32)).astype(BF16)
    run = car_ref[...]
    if carry_bias is not None:
        run = run + carry_bias
    args = [None] * n_sub
    for c in reversed(range(n_sub)):
        cols = slice(c * sub, (c + 1) * sub)
        sums = _dot(jnp.concatenate([hi[:, cols], lo[:, cols]], axis=1), umat)
        args[c] = z[:, cols] + sums[:, :sub] + run
        run = run + sums[:, sub:]
    w = jnp.exp(jnp.concatenate(args, axis=1))
    if vis is not None:
        w = jnp.where(vis, w, 0.0)
    acc_ref[...] += pv(w.astype(BF16), v_blk)
    car_ref[...] = run
    return jnp.max(run)


def _suffix_sum_matrix():
    sub = V7X_LANES
    j = jnp.arange(2 * sub)[:, None] % sub
    s = jnp.arange(2 * sub)[None, :]
    return jnp.where((s >= sub) | (j >= s), -1.0, 0.0).astype(BF16)


def _sb_prompt_kernel(q_ref, kt_ref, v_ref, u_ref, o_ref, acc_ref, car_ref, *, head_dim):
    i = pl.program_id(2)
    tq = q_ref.shape[0]
    tk = kt_ref.shape[-1]
    n_pairs = kt_ref.shape[1]
    lane = lax.broadcasted_iota(jnp.int32, (tq, V7X_LANES), 1)
    row_q = lax.broadcasted_iota(jnp.int32, (2 * tq, tk), 0) & (tq - 1)
    col = lax.broadcasted_iota(jnp.int32, (2 * tq, tk), 1)
    diagonal = col < row_q
    acc_ref[...] = jnp.zeros_like(acc_ref)
    car_ref[...] = jnp.zeros_like(car_ref)
    umat = u_ref[...]
    qm = []
    for p in range(n_pairs):
        q = q_ref[:, p * V7X_LANES:(p + 1) * V7X_LANES]
        zero = jnp.zeros_like(q)
        qm.append(jnp.concatenate([jnp.where(lane < head_dim, q, zero), jnp.where(lane >= head_dim, q, zero)],
                                  axis=0))

    def block(j, vis=None, carry_bias=None):
        cmax = None
        for p in range(n_pairs):
            z = _dot(qm[p], kt_ref[0, p, j])
            v_blk = v_ref[0, j, :, p * V7X_LANES:(p + 1) * V7X_LANES]
            c = _sb_block(z, v_blk, umat, acc_ref.at[p], car_ref.at[p], vis, carry_bias)
            cmax = c if cmax is None else jnp.maximum(cmax, c)
        return cmax

    block(i, vis=diagonal)
    cmax = block(jnp.maximum(i - 1, 0), carry_bias=jnp.where(i > 0, 0.0, NO_BLOCK_BIAS))
    lax.while_loop(lambda s: (s[0] >= 0) & (s[1] > NEG_CUT),
                   lambda s: (s[0] - 1, block(s[0])), (i - 2, cmax))
    for p in range(n_pairs):
        acc = acc_ref[p]
        o_ref[:, p * V7X_LANES:(p + 1) * V7X_LANES] = jnp.where(lane < head_dim, acc[:tq], acc[tq:]).astype(o_ref.dtype)


def _sb_prompt(q, kt, vb, umat, *, batch, head_dim):
    n, d = q.shape
    assert TQ == TK
    seq = n // batch
    n_hp = d // V7X_LANES
    nq = seq // TQ
    nk = seq // TK
    v4 = vb.reshape(batch, nk, TK, d)
    lanes = HP_PER_STEP * V7X_LANES
    return pl.pallas_call(
        functools.partial(_sb_prompt_kernel, head_dim=head_dim),
        grid=(batch, n_hp // HP_PER_STEP, nq),
        in_specs=[pl.BlockSpec((TQ, lanes), lambda b, h, i: (b * nq + i, h)),
                  pl.BlockSpec((1, HP_PER_STEP, nk, V7X_LANES, TK), lambda b, h, i: (b, h, 0, 0, 0)),
                  pl.BlockSpec((1, nk, TK, lanes), lambda b, h, i: (b, 0, 0, h)),
                  _const_spec(umat.shape)],
        out_specs=pl.BlockSpec((TQ, lanes), lambda b, h, i: (b * nq + i, h)),
        out_shape=jax.ShapeDtypeStruct((n, d), BF16),
        scratch_shapes=[pltpu.VMEM((HP_PER_STEP, 2 * TQ, V7X_LANES), F32),
                        pltpu.VMEM((HP_PER_STEP, 2 * TQ, V7X_LANES), F32)],
        compiler_params=_params("arbitrary", "arbitrary", "arbitrary"),
        name="sb_prompt",
    )(q, kt, v4, umat)


def _sb_sample_kernel(q_ref, kn_ref, vn_ref, u_ref, ck_hbm, cv_hbm, o_ref,
                      acc_ref, car_ref, kpad_ref, vpad_ref, kbuf, vbuf, sem, *, head_dim):
    s = pl.program_id(0)
    n_streams = pl.num_programs(0)
    t_new, d = q_ref.shape[1:]
    n_heads = d // head_dim
    rows = n_heads * t_new
    newest = ck_hbm.shape[2] // TK - 1
    slot = s % 2

    def copies(stream, j, into):
        keys = pl.ds(j * TK, TK)
        return (pltpu.make_async_copy(ck_hbm.at[stream, :, keys], kbuf.at[into], sem.at[0, into]),
                pltpu.make_async_copy(cv_hbm.at[stream, :, keys], vbuf.at[into], sem.at[1, into]))

    def start(stream, j, into):
        for c in copies(stream, j, into):
            c.start()

    def wait(stream, j, into):
        for c in copies(stream, j, into):
            c.wait()

    @pl.when(s == 0)
    def _():
        start(s, newest, slot)

    @pl.when(s + 1 < n_streams)
    def _():
        start(s + 1, newest, 1 - slot)

    q = q_ref[0]
    q_rep = jnp.concatenate([q] * n_heads, axis=0)
    row_head = lax.broadcasted_iota(jnp.int32, (rows, d), 0) // t_new
    lane_head = lax.broadcasted_iota(jnp.int32, (rows, d), 1) // head_dim
    qm = jnp.where(row_head == lane_head, q_rep, jnp.zeros_like(q_rep))
    acc_ref[...] = jnp.zeros_like(acc_ref)
    car_ref[...] = jnp.zeros_like(car_ref)
    umat = u_ref[...]

    contract_last = (((1,), (1,)), ((), ()))

    def cache_block():
        z = _dot(qm, kbuf[slot].astype(BF16))
        pv = lambda w, v_t: lax.dot_general(w, v_t, contract_last, preferred_element_type=F32)
        return _sb_block(z, vbuf[slot].astype(BF16), umat, acc_ref, car_ref, pv=pv)

    kpad_ref[...] = jnp.zeros_like(kpad_ref)
    vpad_ref[...] = jnp.zeros_like(vpad_ref)
    kpad_ref[0:t_new, :] = kn_ref[0].astype(BF16)
    vpad_ref[0:t_new, :] = vn_ref[0].astype(BF16)
    n_pad = kpad_ref.shape[0]
    row_q = lax.broadcasted_iota(jnp.int32, (rows, n_pad), 0) % t_new
    col = lax.broadcasted_iota(jnp.int32, (rows, n_pad), 1)
    z_new = lax.dot_general(qm, kpad_ref[...], contract_last, preferred_element_type=F32)
    _sb_block(z_new, vpad_ref[...], umat, acc_ref, car_ref, vis=col < row_q)

    wait(s, newest, slot)
    cmax = cache_block()

    def body(state):
        j, _ = state
        start(s, j, slot)
        wait(s, j, slot)
        return j - 1, cache_block()

    lax.while_loop(lambda st: (st[0] >= 0) & (st[1] > NEG_CUT), body, (newest - 1, cmax))
    acc = acc_ref[...]
    lane_head_o = lax.broadcasted_iota(jnp.int32, (t_new, d), 1) // head_dim
    o = jnp.zeros((t_new, d), F32)
    for h in range(n_heads):
        o = jnp.where(lane_head_o == h, acc[h * t_new:(h + 1) * t_new, :], o)
    o_ref[0] = o.astype(o_ref.dtype)


def _sb_sample(q, k_new, v_new, cache_kt, cache_vt, umat, *, head_dim):
    streams, t_new, d = q.shape
    n_heads = d // head_dim
    assert cache_kt.shape[1] == d and cache_kt.shape[2] % TK == 0
    rows = n_heads * t_new
    new_spec = pl.BlockSpec((1, t_new, d), lambda s: (s, 0, 0))
    hbm_spec = pl.BlockSpec(memory_space=pl.ANY)
    return pl.pallas_call(
        functools.partial(_sb_sample_kernel, head_dim=head_dim),
        grid=(streams,),
        in_specs=[new_spec, new_spec, new_spec, _const_spec(umat.shape), hbm_spec, hbm_spec],
        out_specs=new_spec,
        out_shape=jax.ShapeDtypeStruct((streams, t_new, d), BF16),
        scratch_shapes=[pltpu.VMEM((rows, d), F32), pltpu.VMEM((rows, V7X_LANES), F32),
                        pltpu.VMEM((V7X_LANES, d), BF16), pltpu.VMEM((V7X_LANES, d), BF16),
                        pltpu.VMEM((2, d, TK), F32), pltpu.VMEM((2, d, TK), F32),
                        pltpu.SemaphoreType.DMA((2, 2))],
        compiler_params=_params("arbitrary"),
        name="sb_sample",
    )(q, k_new, v_new, umat, cache_kt, cache_vt)


def kernel(x_prompt, x_sample, cache_k, cache_v, state_conv, ln_g, ln_b, w_ffn_gate, w_ffn_up, w_ffn_down,
           w_conv_in, w_conv, w_conv_out, w_kv, w_q, w_o):
    batch, seq, d = x_prompt.shape
    streams, t_new, _ = x_sample.shape
    depth = ln_g.shape[0]
    n_a = w_conv_in.shape[0]
    head_dim = d // N_HEADS
    alpha = (2.0 * depth) ** 0.25

    x_p = x_prompt.reshape(batch * seq, d)
    x_s = x_sample.reshape(streams * t_new, d)
    zero_prev = jnp.zeros((batch, CONV_W - 1, d), F32)
    past = cache_k.shape[1]
    cache_kt = jnp.transpose(cache_k, (0, 2, 3, 1)).reshape(streams, d, past)
    cache_vt = jnp.transpose(cache_v, (0, 2, 3, 1)).reshape(streams, d, past)
    umat = _suffix_sum_matrix()
    ln_g2 = ln_g.reshape(depth * 3, d)
    ln_b2 = ln_b.reshape(depth * 3, d)
    wg, wu, wd = w_ffn_gate.astype(BF16), w_ffn_up.astype(BF16), w_ffn_down.astype(BF16)
    w_in, w_out = w_conv_in.astype(BF16), w_conv_out.astype(BF16)
    w_kv_b, w_q_b, w_o_b = w_kv.astype(BF16), w_q.astype(BF16), w_o.astype(BF16)
    ffn = functools.partial(_ffn_ln, wg=wg, wu=wu, wd=wd, ln_g=ln_g2, ln_b=ln_b2, alpha=alpha,
                            tm=TM_FFN, ff_chunks=FFN_CHUNKS)

    conv_prompt, conv_sample = [], []
    for l in range(depth):
        if l < n_a:
            x_p, x_s = ffn(x_p, x_s, layer=l, half=0, ln_row=3 * l)
            conv = functools.partial(_conv_ln, win=w_in, wconv=w_conv, wout=w_out, ln_g=ln_g2, ln_b=ln_b2,
                                     layer=l, ln_row=3 * l + 1, alpha=alpha)
            x_p, st = conv(x_p, zero_prev, streams=batch, tm=TM_CONV)
            conv_prompt.append(st)
            x_s, st = conv(x_s, state_conv[l], streams=streams, tm=t_new)
            conv_sample.append(st)
        else:
            i = l - n_a
            x_p, x_s, q_p, q_s = ffn(x_p, x_s, layer=l, half=0, ln_row=3 * l,
                                     proj=_q_projection(w_q_b, index=i, scale=head_dim ** -0.5))
            o_p = _sb_prompt(q_p, kt, vb, umat, batch=batch, head_dim=head_dim)
            o_s = _sb_sample(q_s.reshape(streams, t_new, d),
                             k_sf.reshape(streams, t_new, d), v_sf.reshape(streams, t_new, d),
                             cache_kt, cache_vt, umat, head_dim=head_dim)
            x_p, x_s = _proj_ln(x_p, x_s, o_p, o_s.reshape(streams * t_new, d), w_o_b, ln_g2, ln_b2,
                                index=i, ln_row=3 * l + 1, alpha=alpha)
        if l == n_a - 1:
            x_p, x_s, k_pt, v_pt, kt, vb, k_s, v_s, k_sf, v_sf = ffn(
                x_p, x_s, layer=l, half=1, ln_row=3 * l + 2, tm=TM_FFN_KV,
                proj=_kv_projection(w_kv_b, batch=batch, head_dim=head_dim))
        else:
            x_p, x_s = ffn(x_p, x_s, layer=l, half=1, ln_row=3 * l + 2)

    heads = (N_HEADS, head_dim)
    to_bthd = lambda a_t: jnp.transpose(a_t.reshape(batch, *heads, seq), (0, 3, 1, 2))
    return (x_p.reshape(batch, seq, d),
            x_s.reshape(streams, t_new, d),
            to_bthd(k_pt), to_bthd(v_pt),
            jnp.stack(conv_prompt, axis=0),
            k_s.reshape(streams, t_new, *heads), v_s.reshape(streams, t_new, *heads),
            jnp.stack(conv_sample, axis=0))
```

```python
import functools
from typing import NamedTuple

import jax
import jax.numpy as jnp
from jax import lax
from jax.experimental import pallas as pl
from jax.experimental.pallas import tpu as pltpu

F32 = jnp.float32
BF16 = jnp.bfloat16

N_HEADS = 16
CONV_W = 3
LN_EPS = 1e-5

V7X_LANES = 128
BF16_SUBLANES = 16
VMEM_LIMIT_BYTES = 56 * 1024 * 1024

TM = 256
TM_FFN = 256
HP_PER_STEP = 4
TQ = 256
TK = 256
TM_CONV = 512

NEG_CUT = -110.0
NO_BLOCK_BIAS = -1e30
SOFTPLUS_LINEAR = 80.0


def _dot(a, b):
    return jnp.dot(a, b, preferred_element_type=F32)


def _layer_norm(r, g, b):
    mu = jnp.mean(r, axis=-1, keepdims=True)
    xc = r - mu
    var = jnp.mean(xc * xc, axis=-1, keepdims=True)
    return xc * lax.rsqrt(var + LN_EPS) * g + b


def _params(*sem):
    return pltpu.CompilerParams(dimension_semantics=sem, vmem_limit_bytes=VMEM_LIMIT_BYTES)


def _const_spec(shape, index=None):
    if index is None:
        return pl.BlockSpec(shape, lambda *_: (0,) * len(shape), pipeline_mode=pl.Buffered(1))
    block = (None,) * len(index) + tuple(shape)
    full = tuple(index) + (0,) * len(shape)
    return pl.BlockSpec(block, lambda *_: full, pipeline_mode=pl.Buffered(1))


def _pair_specs(n_prompt, d):
    npt = n_prompt // TM
    prompt = pl.BlockSpec((TM, d), lambda i: (jnp.minimum(i, npt - 1), 0))
    sample = pl.BlockSpec((TM, d), lambda i: (0, 0))
    return npt, prompt, sample


def _pair_load(p_ref, s_ref, npt):
    return jnp.where(pl.program_id(0) < npt, p_ref[...], s_ref[...])


def _pair_store(p_ref, s_ref, npt, value):
    @pl.when(pl.program_id(0) < npt)
    def _():
        p_ref[...] = value.astype(p_ref.dtype)

    @pl.when(pl.program_id(0) >= npt)
    def _():
        s_ref[...] = value.astype(s_ref.dtype)


def _ffn_ln_kernel(*refs, alpha, ln_row, npt, n_proj_w, n_proj_prompt, n_cast, proj_prompt, proj_sample):
    xp_ref, xs_ref, wg_ref, wu_ref, wd_ref, g_ref, b_ref = refs[:7]
    proj_w = refs[7:7 + n_proj_w]
    cast_in = refs[7 + n_proj_w:7 + n_proj_w + n_cast]
    outs = refs[7 + n_proj_w + n_cast:-2]
    op_ref, os_ref = outs[:2]
    prompt_out = outs[2:2 + n_proj_prompt]
    sample_out = outs[2 + n_proj_prompt:len(outs) - n_cast]
    cast_out = outs[len(outs) - n_cast:]
    pre_ref, pre_s_ref = refs[-2:]
    s = pl.program_id(0)

    for src, dst in zip(cast_in, cast_out):
        dst[...] = src[...].astype(dst.dtype)

    def pre_norm(x):
        xb = x.astype(BF16)
        gate = _dot(xb, wg_ref[...])
        up = _dot(xb, wu_ref[...])
        return alpha * x + 0.5 * _dot((gate * jax.nn.sigmoid(gate) * up).astype(BF16), wd_ref[...])

    def finish(pre, x_ref, proj, out_refs):
        xn = _layer_norm(pre, g_ref[ln_row:ln_row + 1, :], b_ref[ln_row:ln_row + 1, :])
        x_ref[...] = xn
        if proj is not None:
            proj(xn, proj_w, out_refs)

    @pl.when(s == 0)
    def _():
        pre_ref[1] = jnp.zeros(pre_ref.shape[1:], F32)
        pre_s_ref[...] = pre_norm(xs_ref[...])

    @pl.when((s >= 1) & (s <= npt))
    def _():
        finish(pre_ref[s % 2], op_ref, proj_prompt, prompt_out)
        pre_ref[(s + 1) % 2] = pre_norm(xp_ref[...])

    @pl.when(s == 1)
    def _():
        finish(pre_s_ref[...], os_ref, proj_sample, sample_out)

    @pl.when(s == npt + 1)
    def _():
        finish(pre_ref[(npt + 1) % 2], op_ref, proj_prompt, prompt_out)


def _ffn_ln(x_p, x_s, weights, ln_g, ln_b, *, ln_row, alpha, tm, proj=None, cast_next=None):
    n_prompt, d = x_p.shape
    ts = x_s.shape[0]
    dff = weights[0].shape[1]
    npt = n_prompt // tm
    steps = npt + 2
    in_tile = lambda s: jnp.clip(s - 1, 0, npt - 1)
    out_tile = lambda s: jnp.clip(s - 2, 0, npt - 1)
    sample = pl.BlockSpec((ts, d), lambda s: (0, 0))
    prompt_out = pl.BlockSpec((tm, d), lambda s: (out_tile(s), 0))
    if proj is None:
        proj = _Projection((), (), (), (), (), (), None, None)
    else:
        proj = proj(n_prompt, ts, d, tm, out_tile)
    cast_w, cast_in_specs, cast_out_specs, cast_shapes = (), [], [], []
    if cast_next is not None:
        *cast_w, layer, half = cast_next
        for w in cast_w:
            rows, cols = w.shape[2:]
            slab = _cast_slab_rows(rows, steps)
            slab_of = functools.partial(lambda s, last: jnp.minimum(s, last), last=rows // slab - 1)
            cast_in_specs.append(pl.BlockSpec((None, None, slab, cols),
                                              lambda s, slab_of=slab_of: (layer, half, slab_of(s), 0)))
            cast_out_specs.append(pl.BlockSpec((slab, cols), lambda s, slab_of=slab_of: (slab_of(s), 0)))
            cast_shapes.append(jax.ShapeDtypeStruct((rows, cols), BF16))
    outs = pl.pallas_call(
        functools.partial(_ffn_ln_kernel, alpha=alpha, ln_row=ln_row, npt=npt,
                          n_proj_w=len(proj.weights), n_proj_prompt=len(proj.prompt_shapes), n_cast=len(cast_w),
                          proj_prompt=proj.prompt_fn, proj_sample=proj.sample_fn),
        grid=(steps,),
        in_specs=[pl.BlockSpec((tm, d), lambda s: (in_tile(s), 0)), sample,
                  _const_spec((d, dff)), _const_spec((d, dff)), _const_spec((dff, d)),
                  _const_spec(ln_g.shape), _const_spec(ln_b.shape), *proj.weight_specs, *cast_in_specs],
        out_specs=[prompt_out, sample, *proj.prompt_specs, *proj.sample_specs, *cast_out_specs],
        out_shape=[jax.ShapeDtypeStruct(x_p.shape, F32), jax.ShapeDtypeStruct(x_s.shape, F32),
                   *proj.prompt_shapes, *proj.sample_shapes, *cast_shapes],
        scratch_shapes=[pltpu.VMEM((2, tm, d), F32), pltpu.VMEM((ts, d), F32)],
        compiler_params=_params("arbitrary"),
        name="ffn_ln",
    )(x_p, x_s, *weights, ln_g, ln_b, *proj.weights, *cast_w)
    return outs


def _cast_slab_rows(rows, steps):
    slab = BF16_SUBLANES
    while rows % slab or rows // slab > steps:
        slab += BF16_SUBLANES
    return slab


class _Projection(NamedTuple):
    weights: tuple
    weight_specs: tuple
    prompt_shapes: tuple
    prompt_specs: tuple
    sample_shapes: tuple
    sample_specs: tuple
    prompt_fn: object
    sample_fn: object


def _q_projection(wq, *, index, scale):
    def store_q(xn, w_refs, out_refs):
        out_refs[0][...] = (_dot(xn.astype(BF16), w_refs[0][...]) * scale).astype(BF16)

    def build(n_prompt, n_sample, d, tm, out_tile):
        return _Projection(
            (wq,), (_const_spec((d, d), (index,)),),
            (jax.ShapeDtypeStruct((n_prompt, d), BF16),), (pl.BlockSpec((tm, d), lambda s: (out_tile(s), 0)),),
            (jax.ShapeDtypeStruct((n_sample, d), BF16),), (pl.BlockSpec((n_sample, d), lambda s: (0, 0)),),
            store_q, store_q)
    return build


def _store_heads(ref, value):
    head_dim = ref.shape[2]
    for h in range(ref.shape[1]):
        ref[:, h, :] = value[:, h * head_dim:(h + 1) * head_dim]


def _kv_projection(w_kv, *, batch, head_dim):
    def kv(xn, w_refs):
        xb = xn.astype(BF16)
        return _dot(xb, w_refs[0][...]), _dot(xb, w_refs[1][...])

    def store_prompt(xn, w_refs, out_refs):
        kpt_ref, vpt_ref, kt_ref, vb_ref = out_refs
        k, v = kv(xn, w_refs)
        vb_ref[...] = v.astype(BF16)
        for hp in range(kt_ref.shape[1]):
            lanes = slice(hp * V7X_LANES, (hp + 1) * V7X_LANES)
            k_t = k[:, lanes].T
            kpt_ref[0, lanes, :] = k_t
            for c in range(kt_ref.shape[2]):
                kt_ref[0, hp, c] = k_t[:, c * TK:(c + 1) * TK].astype(BF16)
            vpt_ref[0, lanes, :] = v[:, lanes].T

    def store_sample(xn, w_refs, out_refs):
        ks_ref, vs_ref, ksf_ref, vsf_ref = out_refs
        k, v = kv(xn, w_refs)
        _store_heads(ks_ref, k)
        _store_heads(vs_ref, v)
        ksf_ref[...] = k
        vsf_ref[...] = v

    def build(n_prompt, n_sample, d, tm, out_tile):
        seq = n_prompt // batch
        tiles_per_stream = seq // tm
        blocks_per_tile = tm // TK
        n_hp = d // V7X_LANES
        heads = (d // head_dim, head_dim)
        where = lambda s: (out_tile(s) // tiles_per_stream, out_tile(s) % tiles_per_stream)
        prompt_t = pl.BlockSpec((1, d, tm), lambda s: (where(s)[0], 0, where(s)[1]))
        kt_spec = pl.BlockSpec((1, n_hp, blocks_per_tile, V7X_LANES, TK),
                               lambda s: (where(s)[0], 0, where(s)[1], 0, 0))
        sample4 = pl.BlockSpec((n_sample, *heads), lambda s: (0, 0, 0))
        sample2 = pl.BlockSpec((n_sample, d), lambda s: (0, 0))
        w_spec = lambda half: pl.BlockSpec((d, d), lambda s: (0, half), pipeline_mode=pl.Buffered(1))
        return _Projection(
            (w_kv, w_kv), (w_spec(0), w_spec(1)),
            (jax.ShapeDtypeStruct((batch, d, seq), F32), jax.ShapeDtypeStruct((batch, d, seq), F32),
             jax.ShapeDtypeStruct((batch, n_hp, seq // TK, V7X_LANES, TK), BF16),
             jax.ShapeDtypeStruct((n_prompt, d), BF16)),
            (prompt_t, prompt_t, kt_spec, pl.BlockSpec((tm, d), lambda s: (out_tile(s), 0))),
            (jax.ShapeDtypeStruct((n_sample, *heads), F32), jax.ShapeDtypeStruct((n_sample, *heads), F32),
             jax.ShapeDtypeStruct((n_sample, d), F32), jax.ShapeDtypeStruct((n_sample, d), F32)),
            (sample4, sample4, sample2, sample2),
            store_prompt, store_sample)
    return build


def _proj_ln_kernel(xp_ref, xs_ref, op_ref, os_ref, wo_ref, g_ref, b_ref, yp_ref, ys_ref, *, alpha, ln_row, npt):
    y = _dot(_pair_load(op_ref, os_ref, npt), wo_ref[...])
    x = _pair_load(xp_ref, xs_ref, npt)
    out = _layer_norm(alpha * x + y, g_ref[ln_row:ln_row + 1, :], b_ref[ln_row:ln_row + 1, :])
    _pair_store(yp_ref, ys_ref, npt, out)


def _proj_ln(x_p, x_s, o_p, o_s, wo, ln_g, ln_b, *, index, ln_row, alpha):
    n_prompt, d = x_p.shape
    npt, prompt, sample = _pair_specs(n_prompt, d)
    return pl.pallas_call(
        functools.partial(_proj_ln_kernel, alpha=alpha, ln_row=ln_row, npt=npt),
        grid=(npt + 1,),
        in_specs=[prompt, sample, prompt, sample, _const_spec((d, d), (index,)),
                  _const_spec(ln_g.shape), _const_spec(ln_b.shape)],
        out_specs=[prompt, sample],
        out_shape=[jax.ShapeDtypeStruct(x_p.shape, F32), jax.ShapeDtypeStruct(x_s.shape, F32)],
        compiler_params=_params("arbitrary"),
        name="proj_ln",
    )(x_p, x_s, o_p, o_s, wo, ln_g, ln_b)


def _conv_ln_kernel(x_ref, prev_ref, win_ref, wconv_ref, wout_ref, g_ref, b_ref,
                    o_ref, st_ref, car_ref, *, alpha, layer, ln_row):
    tm, d = x_ref.shape

    @pl.when(pl.program_id(1) == 0)
    def _():
        car_ref[...] = prev_ref[0]

    x = x_ref[...]
    p = _dot(x.astype(BF16), win_ref[...])
    gate_out = p[:, :d]
    u = p[:, d:2 * d] * p[:, 2 * d:]
    prev2 = car_ref[0:1, :]
    prev1 = car_ref[1:2, :]
    row = lax.broadcasted_iota(jnp.int32, (tm, d), 0)
    u1 = jnp.where(row == 0, prev1, pltpu.roll(u, 1, 0))
    u2 = jnp.where(row == 0, prev2, jnp.where(row == 1, prev1, pltpu.roll(u, 2, 0)))
    wc = wconv_ref[layer]
    conv = wc[0:1, :] * u2 + wc[1:2, :] * u1 + wc[2:3, :] * u
    y = _dot((gate_out * conv).astype(BF16), wout_ref[...])
    o_ref[...] = _layer_norm(alpha * x + y, g_ref[ln_row:ln_row + 1, :], b_ref[ln_row:ln_row + 1, :])
    last = u[tm - (CONV_W - 1):, :]
    car_ref[...] = last
    st_ref[0] = last


def _conv_ln(x, prev, win, wconv, wout, ln_g, ln_b, *, layer, ln_row, alpha, streams, tm):
    n, d = x.shape
    nt = n // streams // tm
    x_spec = pl.BlockSpec((tm, d), lambda s, t: (s * nt + t, 0))
    st_spec = pl.BlockSpec((1, CONV_W - 1, d), lambda s, t: (s, 0, 0))
    return pl.pallas_call(
        functools.partial(_conv_ln_kernel, alpha=alpha, layer=layer, ln_row=ln_row),
        grid=(streams, nt),
        in_specs=[x_spec, st_spec,
                  _const_spec((d, 3 * d), (layer,)), _const_spec(wconv.shape), _const_spec((d, d), (layer,)),
                  _const_spec(ln_g.shape), _const_spec(ln_b.shape)],
        out_specs=[x_spec, st_spec],
        out_shape=[jax.ShapeDtypeStruct((n, d), F32),
                   jax.ShapeDtypeStruct((streams, CONV_W - 1, d), F32)],
        scratch_shapes=[pltpu.VMEM((CONV_W - 1, d), F32)],
        compiler_params=_params("arbitrary", "arbitrary"),
        name="conv_ln",
    )(x, prev, win, wconv, wout, ln_g, ln_b)


def _sb_block(z, v_blk, umat, acc_ref, car_ref, vis=None, carry_bias=None, pv=_dot):
    sub = V7X_LANES
    n_sub = z.shape[1] // sub
    softplus = jnp.maximum(z, jnp.log(1.0 + jnp.exp(jnp.minimum(z, SOFTPLUS_LINEAR))))
    if vis is not None:
        softplus = jnp.where(vis, softplus, 0.0)
    hi = softplus.astype(BF16)
    lo = (softplus - hi.astype(F32)).astype(BF16)
    run = car_ref[...]
    if carry_bias is not None:
        run = run + carry_bias
    args = [None] * n_sub
    for c in reversed(range(n_sub)):
        cols = slice(c * sub, (c + 1) * sub)
        sums = _dot(jnp.concatenate([hi[:, cols], lo[:, cols]], axis=1), umat)
        args[c] = z[:, cols] + sums[:, :sub] + run
        run = run + sums[:, sub:]
    w = jnp.exp(jnp.concatenate(args, axis=1))
    if vis is not None:
        w = jnp.where(vis, w, 0.0)
    acc_ref[...] += pv(w.astype(BF16), v_blk)
    car_ref[...] = run
    return jnp.max(run)


def _suffix_sum_matrix():
    sub = V7X_LANES
    j = jnp.arange(2 * sub)[:, None] % sub
    s = jnp.arange(2 * sub)[None, :]
    return jnp.where((s >= sub) | (j >= s), -1.0, 0.0).astype(BF16)


def _sb_prompt_kernel(q_ref, kt_ref, v_ref, u_ref, o_ref, acc_ref, car_ref, *, head_dim):
    i = pl.program_id(2)
    tq = q_ref.shape[0]
    tk = kt_ref.shape[-1]
    n_pairs = kt_ref.shape[1]
    lane = lax.broadcasted_iota(jnp.int32, (tq, V7X_LANES), 1)
    row_q = lax.broadcasted_iota(jnp.int32, (2 * tq, tk), 0) & (tq - 1)
    col = lax.broadcasted_iota(jnp.int32, (2 * tq, tk), 1)
    diagonal = col < row_q
    acc_ref[...] = jnp.zeros_like(acc_ref)
    car_ref[...] = jnp.zeros_like(car_ref)
    umat = u_ref[...]
    qm = []
    for p in range(n_pairs):
        q = q_ref[:, p * V7X_LANES:(p + 1) * V7X_LANES]
        zero = jnp.zeros_like(q)
        qm.append(jnp.concatenate([jnp.where(lane < head_dim, q, zero), jnp.where(lane >= head_dim, q, zero)],
                                  axis=0))

    def block(j, vis=None, carry_bias=None):
        cmax = None
        for p in range(n_pairs):
            z = _dot(qm[p], kt_ref[0, p, j])
            v_blk = v_ref[0, j, :, p * V7X_LANES:(p + 1) * V7X_LANES]
            c = _sb_block(z, v_blk, umat, acc_ref.at[p], car_ref.at[p], vis, carry_bias)
            cmax = c if cmax is None else jnp.maximum(cmax, c)
        return cmax

    block(i, vis=diagonal)
    cmax = block(jnp.maximum(i - 1, 0), carry_bias=jnp.where(i > 0, 0.0, NO_BLOCK_BIAS))
    lax.while_loop(lambda s: (s[0] >= 0) & (s[1] > NEG_CUT),
                   lambda s: (s[0] - 1, block(s[0])), (i - 2, cmax))
    for p in range(n_pairs):
        acc = acc_ref[p]
        o_ref[:, p * V7X_LANES:(p + 1) * V7X_LANES] = jnp.where(lane < head_dim, acc[:tq], acc[tq:]).astype(o_ref.dtype)


def _sb_prompt(q, kt, vb, umat, *, batch, head_dim):
    n, d = q.shape
    assert TQ == TK
    seq = n // batch
    n_hp = d // V7X_LANES
    nq = seq // TQ
    nk = seq // TK
    v4 = vb.reshape(batch, nk, TK, d)
    lanes = HP_PER_STEP * V7X_LANES
    return pl.pallas_call(
        functools.partial(_sb_prompt_kernel, head_dim=head_dim),
        grid=(batch, n_hp // HP_PER_STEP, nq),
        in_specs=[pl.BlockSpec((TQ, lanes), lambda b, h, i: (b * nq + i, h)),
                  pl.BlockSpec((1, HP_PER_STEP, nk, V7X_LANES, TK), lambda b, h, i: (b, h, 0, 0, 0)),
                  pl.BlockSpec((1, nk, TK, lanes), lambda b, h, i: (b, 0, 0, h)),
                  _const_spec(umat.shape)],
        out_specs=pl.BlockSpec((TQ, lanes), lambda b, h, i: (b * nq + i, h)),
        out_shape=jax.ShapeDtypeStruct((n, d), BF16),
        scratch_shapes=[pltpu.VMEM((HP_PER_STEP, 2 * TQ, V7X_LANES), F32),
                        pltpu.VMEM((HP_PER_STEP, 2 * TQ, V7X_LANES), F32)],
        compiler_params=_params("arbitrary", "arbitrary", "arbitrary"),
        name="sb_prompt",
    )(q, kt, v4, umat)


def _sb_sample_kernel(q_ref, kn_ref, vn_ref, u_ref, ck_hbm, cv_hbm, o_ref,
                      acc_ref, car_ref, kpad_ref, vpad_ref, kbuf, vbuf, sem, *, head_dim):
    s = pl.program_id(0)
    n_streams = pl.num_programs(0)
    t_new, d = q_ref.shape[1:]
    n_heads = d // head_dim
    rows = n_heads * t_new
    newest = ck_hbm.shape[2] // TK - 1
    slot = s % 2

    def copies(stream, j, into):
        keys = pl.ds(j * TK, TK)
        return (pltpu.make_async_copy(ck_hbm.at[stream, :, keys], kbuf.at[into], sem.at[0, into]),
                pltpu.make_async_copy(cv_hbm.at[stream, :, keys], vbuf.at[into], sem.at[1, into]))

    def start(stream, j, into):
        for c in copies(stream, j, into):
            c.start()

    def wait(stream, j, into):
        for c in copies(stream, j, into):
            c.wait()

    @pl.when(s == 0)
    def _():
        start(s, newest, slot)

    @pl.when(s + 1 < n_streams)
    def _():
        start(s + 1, newest, 1 - slot)

    q = q_ref[0]
    q_rep = jnp.concatenate([q] * n_heads, axis=0)
    row_head = lax.broadcasted_iota(jnp.int32, (rows, d), 0) // t_new
    lane_head = lax.broadcasted_iota(jnp.int32, (rows, d), 1) // head_dim
    qm = jnp.where(row_head == lane_head, q_rep, jnp.zeros_like(q_rep))
    acc_ref[...] = jnp.zeros_like(acc_ref)
    car_ref[...] = jnp.zeros_like(car_ref)
    umat = u_ref[...]

    contract_last = (((1,), (1,)), ((), ()))

    def cache_block():
        z = _dot(qm, kbuf[slot].astype(BF16))
        pv = lambda w, v_t: lax.dot_general(w, v_t, contract_last, preferred_element_type=F32)
        return _sb_block(z, vbuf[slot].astype(BF16), umat, acc_ref, car_ref, pv=pv)

    kpad_ref[...] = jnp.zeros_like(kpad_ref)
    vpad_ref[...] = jnp.zeros_like(vpad_ref)
    kpad_ref[0:t_new, :] = kn_ref[0].astype(BF16)
    vpad_ref[0:t_new, :] = vn_ref[0].astype(BF16)
    n_pad = kpad_ref.shape[0]
    row_q = lax.broadcasted_iota(jnp.int32, (rows, n_pad), 0) % t_new
    col = lax.broadcasted_iota(jnp.int32, (rows, n_pad), 1)
    z_new = lax.dot_general(qm, kpad_ref[...], contract_last, preferred_element_type=F32)
    _sb_block(z_new, vpad_ref[...], umat, acc_ref, car_ref, vis=col < row_q)

    wait(s, newest, slot)
    cmax = cache_block()

    def body(state):
        j, _ = state
        start(s, j, slot)
        wait(s, j, slot)
        return j - 1, cache_block()

    lax.while_loop(lambda st: (st[0] >= 0) & (st[1] > NEG_CUT), body, (newest - 1, cmax))
    acc = acc_ref[...]
    lane_head_o = lax.broadcasted_iota(jnp.int32, (t_new, d), 1) // head_dim
    o = jnp.zeros((t_new, d), F32)
    for h in range(n_heads):
        o = jnp.where(lane_head_o == h, acc[h * t_new:(h + 1) * t_new, :], o)
    o_ref[0] = o.astype(o_ref.dtype)


def _sb_sample(q, k_new, v_new, cache_kt, cache_vt, umat, *, head_dim):
    streams, t_new, d = q.shape
    n_heads = d // head_dim
    assert cache_kt.shape[1] == d and cache_kt.shape[2] % TK == 0
    rows = n_heads * t_new
    new_spec = pl.BlockSpec((1, t_new, d), lambda s: (s, 0, 0))
    hbm_spec = pl.BlockSpec(memory_space=pl.ANY)
    return pl.pallas_call(
        functools.partial(_sb_sample_kernel, head_dim=head_dim),
        grid=(streams,),
        in_specs=[new_spec, new_spec, new_spec, _const_spec(umat.shape), hbm_spec, hbm_spec],
        out_specs=new_spec,
        out_shape=jax.ShapeDtypeStruct((streams, t_new, d), BF16),
        scratch_shapes=[pltpu.VMEM((rows, d), F32), pltpu.VMEM((rows, V7X_LANES), F32),
                        pltpu.VMEM((V7X_LANES, d), BF16), pltpu.VMEM((V7X_LANES, d), BF16),
                        pltpu.VMEM((2, d, TK), F32), pltpu.VMEM((2, d, TK), F32),
                        pltpu.SemaphoreType.DMA((2, 2))],
        compiler_params=_params("arbitrary"),
        name="sb_sample",
    )(q, k_new, v_new, umat, cache_kt, cache_vt)


def kernel(x_prompt, x_sample, cache_k, cache_v, state_conv, ln_g, ln_b, w_ffn_gate, w_ffn_up, w_ffn_down,
           w_conv_in, w_conv, w_conv_out, w_kv, w_q, w_o):
    batch, seq, d = x_prompt.shape
    streams, t_new, _ = x_sample.shape
    depth = ln_g.shape[0]
    n_a = w_conv_in.shape[0]
    head_dim = d // N_HEADS
    alpha = (2.0 * depth) ** 0.25

    x_p = x_prompt.reshape(batch * seq, d)
    x_s = x_sample.reshape(streams * t_new, d)
    zero_prev = jnp.zeros((batch, CONV_W - 1, d), F32)
    past = cache_k.shape[1]
    cache_kt = jnp.transpose(cache_k, (0, 2, 3, 1)).reshape(streams, d, past)
    cache_vt = jnp.transpose(cache_v, (0, 2, 3, 1)).reshape(streams, d, past)
    umat = _suffix_sum_matrix()
    ln_g2 = ln_g.reshape(depth * 3, d)
    ln_b2 = ln_b.reshape(depth * 3, d)
    w_in, w_out = w_conv_in.astype(BF16), w_conv_out.astype(BF16)
    w_kv_b, w_q_b, w_o_b = w_kv.astype(BF16), w_q.astype(BF16), w_o.astype(BF16)
    ffn_stacks = (w_ffn_gate, w_ffn_up, w_ffn_down)
    w_ffn = tuple(w[0, 0].astype(BF16) for w in ffn_stacks)

    def ffn(x_p, x_s, weights, layer, half, proj=None):
        nxt = (layer, 1) if half == 0 else (layer + 1, 0)
        cast_next = (*ffn_stacks, *nxt) if nxt[0] < depth else None
        outs = _ffn_ln(x_p, x_s, weights, ln_g2, ln_b2, ln_row=3 * layer + 2 * half, alpha=alpha, tm=TM_FFN,
                       proj=proj, cast_next=cast_next)
        n_keep = len(outs) - (len(ffn_stacks) if cast_next else 0)
        return tuple(outs[:n_keep]), tuple(outs[n_keep:])

    conv_prompt, conv_sample = [], []
    for l in range(depth):
        if l < n_a:
            (x_p, x_s), w_ffn = ffn(x_p, x_s, w_ffn, l, 0)
            conv = functools.partial(_conv_ln, win=w_in, wconv=w_conv, wout=w_out, ln_g=ln_g2, ln_b=ln_b2,
                                     layer=l, ln_row=3 * l + 1, alpha=alpha)
            x_p, st = conv(x_p, zero_prev, streams=batch, tm=TM_CONV)
            conv_prompt.append(st)
            x_s, st = conv(x_s, state_conv[l], streams=streams, tm=t_new)
            conv_sample.append(st)
        else:
            i = l - n_a
            (x_p, x_s, q_p, q_s), w_ffn = ffn(x_p, x_s, w_ffn, l, 0,
                                              proj=_q_projection(w_q_b, index=i, scale=head_dim ** -0.5))
            o_p = _sb_prompt(q_p, kt, vb, umat, batch=batch, head_dim=head_dim)
            o_s = _sb_sample(q_s.reshape(streams, t_new, d),
                             k_sf.reshape(streams, t_new, d), v_sf.reshape(streams, t_new, d),
                             cache_kt, cache_vt, umat, head_dim=head_dim)
            x_p, x_s = _proj_ln(x_p, x_s, o_p, o_s.reshape(streams * t_new, d), w_o_b, ln_g2, ln_b2,
                                index=i, ln_row=3 * l + 1, alpha=alpha)
        if l == n_a - 1:
            (x_p, x_s, k_pt, v_pt, kt, vb, k_s, v_s, k_sf, v_sf), w_ffn = ffn(
                x_p, x_s, w_ffn, l, 1, proj=_kv_projection(w_kv_b, batch=batch, head_dim=head_dim))
        else:
            (x_p, x_s), w_ffn = ffn(x_p, x_s, w_ffn, l, 1)

    heads = (N_HEADS, head_dim)
    to_bthd = lambda a_t: jnp.transpose(a_t.reshape(batch, *heads, seq), (0, 3, 1, 2))
    return (x_p.reshape(batch, seq, d),
            x_s.reshape(streams, t_new, d),
            to_bthd(k_pt), to_bthd(v_pt),
            jnp.stack(conv_prompt, axis=0),
            k_s.reshape(streams, t_new, *heads), v_s.reshape(streams, t_new, *heads),
            jnp.stack(conv_sample, axis=0))
```

```python
import functools
from typing import NamedTuple

import jax
import jax.numpy as jnp
from jax import lax
from jax.experimental import pallas as pl
from jax.experimental.pallas import tpu as pltpu

F32 = jnp.float32
BF16 = jnp.bfloat16

N_HEADS = 16
CONV_W = 3
LN_EPS = 1e-5

V7X_LANES = 128
BF16_SUBLANES = 16
VMEM_LIMIT_BYTES = 56 * 1024 * 1024

TM_FFN = 256
HP_PER_STEP = 4
TQ = 256
TK = 256
TM_CONV = 512

NEG_CUT = -110.0
NO_BLOCK_BIAS = -1e30
SOFTPLUS_LINEAR = 80.0


def _dot(a, b):
    return jnp.dot(a, b, preferred_element_type=F32)


def _layer_norm(r, g, b):
    mu = jnp.mean(r, axis=-1, keepdims=True)
    xc = r - mu
    var = jnp.mean(xc * xc, axis=-1, keepdims=True)
    return xc * lax.rsqrt(var + LN_EPS) * g + b


def _params(*sem):
    return pltpu.CompilerParams(dimension_semantics=sem, vmem_limit_bytes=VMEM_LIMIT_BYTES)


def _const_spec(shape, index=None):
    if index is None:
        return pl.BlockSpec(shape, lambda *_: (0,) * len(shape), pipeline_mode=pl.Buffered(1))
    block = (None,) * len(index) + tuple(shape)
    full = tuple(index) + (0,) * len(shape)
    return pl.BlockSpec(block, lambda *_: full, pipeline_mode=pl.Buffered(1))


def _ffn_ln_kernel(*refs, alpha, ln_row, mix_ln_row, npt, n_proj_w, n_proj_prompt, n_cast, proj_prompt,
                   proj_sample):
    xp_ref, xs_ref = refs[:2]
    if mix_ln_row is not None:
        mp_ref, ms_ref, wo_ref = refs[2:5]
        refs = refs[:2] + refs[5:]
    wg_ref, wu_ref, wd_ref, g_ref, b_ref = refs[2:7]
    proj_w = refs[7:7 + n_proj_w]
    cast_in = refs[7 + n_proj_w:7 + n_proj_w + n_cast]
    outs = refs[7 + n_proj_w + n_cast:-2]
    op_ref, os_ref = outs[:2]
    prompt_out = outs[2:2 + n_proj_prompt]
    sample_out = outs[2 + n_proj_prompt:len(outs) - n_cast]
    cast_out = outs[len(outs) - n_cast:]
    pre_ref, pre_s_ref = refs[-2:]
    s = pl.program_id(0)

    for src, dst in zip(cast_in, cast_out):
        dst[...] = src[...].astype(dst.dtype)

    def pre_norm(x_ref, mix_ref):
        x = x_ref[...]
        if mix_ln_row is not None:
            x = _layer_norm(alpha * x + _dot(mix_ref[...], wo_ref[...]),
                            g_ref[mix_ln_row:mix_ln_row + 1, :], b_ref[mix_ln_row:mix_ln_row + 1, :])
        xb = x.astype(BF16)
        gate = _dot(xb, wg_ref[...])
        up = _dot(xb, wu_ref[...])
        return alpha * x + 0.5 * _dot((gate * jax.nn.sigmoid(gate) * up).astype(BF16), wd_ref[...])

    def finish(pre, x_ref, proj, out_refs):
        xn = _layer_norm(pre, g_ref[ln_row:ln_row + 1, :], b_ref[ln_row:ln_row + 1, :])
        x_ref[...] = xn
        if proj is not None:
            proj(xn, proj_w, out_refs)

    @pl.when(s == 0)
    def _():
        pre_ref[1] = jnp.zeros(pre_ref.shape[1:], F32)
        pre_s_ref[...] = pre_norm(xs_ref, ms_ref if mix_ln_row is not None else None)

    @pl.when((s >= 1) & (s <= npt))
    def _():
        finish(pre_ref[s % 2], op_ref, proj_prompt, prompt_out)
        pre_ref[(s + 1) % 2] = pre_norm(xp_ref, mp_ref if mix_ln_row is not None else None)

    @pl.when(s == 1)
    def _():
        finish(pre_s_ref[...], os_ref, proj_sample, sample_out)

    @pl.when(s == npt + 1)
    def _():
        finish(pre_ref[(npt + 1) % 2], op_ref, proj_prompt, prompt_out)


def _ffn_ln(x_p, x_s, weights, ln_g, ln_b, *, ln_row, alpha, tm, mix=None, proj=None, cast_next=None):
    mix_args, mix_specs, mix_ln_row = (), (), None
    if mix is not None:
        mix_p, mix_s, wo, wo_index, mix_ln_row = mix
        mix_args = (mix_p, mix_s, wo)
    n_prompt, d = x_p.shape
    ts = x_s.shape[0]
    dff = weights[0].shape[1]
    npt = n_prompt // tm
    steps = npt + 2
    in_tile = lambda s: jnp.clip(s - 1, 0, npt - 1)
    out_tile = lambda s: jnp.clip(s - 2, 0, npt - 1)
    sample = pl.BlockSpec((ts, d), lambda s: (0, 0))
    prompt_out = pl.BlockSpec((tm, d), lambda s: (out_tile(s), 0))
    prompt_in = pl.BlockSpec((tm, d), lambda s: (in_tile(s), 0))
    if mix is not None:
        mix_specs = (prompt_in, sample, _const_spec((d, d), (wo_index,)))
    if proj is None:
        proj = _Projection((), (), (), (), (), (), None, None)
    else:
        proj = proj(n_prompt, ts, d, tm, out_tile)
    cast_w, cast_in_specs, cast_out_specs, cast_shapes = (), [], [], []
    if cast_next is not None:
        *cast_w, layer, half = cast_next
        for w in cast_w:
            rows, cols = w.shape[2:]
            slab = _cast_slab_rows(rows, steps)
            slab_of = functools.partial(lambda s, last: jnp.minimum(s, last), last=rows // slab - 1)
            cast_in_specs.append(pl.BlockSpec((None, None, slab, cols),
                                              lambda s, slab_of=slab_of: (layer, half, slab_of(s), 0)))
            cast_out_specs.append(pl.BlockSpec((slab, cols), lambda s, slab_of=slab_of: (slab_of(s), 0)))
            cast_shapes.append(jax.ShapeDtypeStruct((rows, cols), BF16))
    outs = pl.pallas_call(
        functools.partial(_ffn_ln_kernel, alpha=alpha, ln_row=ln_row, mix_ln_row=mix_ln_row, npt=npt,
                          n_proj_w=len(proj.weights), n_proj_prompt=len(proj.prompt_shapes), n_cast=len(cast_w),
                          proj_prompt=proj.prompt_fn, proj_sample=proj.sample_fn),
        grid=(steps,),
        in_specs=[prompt_in, sample, *mix_specs,
                  _const_spec((d, dff)), _const_spec((d, dff)), _const_spec((dff, d)),
                  _const_spec(ln_g.shape), _const_spec(ln_b.shape), *proj.weight_specs, *cast_in_specs],
        out_specs=[prompt_out, sample, *proj.prompt_specs, *proj.sample_specs, *cast_out_specs],
        out_shape=[jax.ShapeDtypeStruct(x_p.shape, F32), jax.ShapeDtypeStruct(x_s.shape, F32),
                   *proj.prompt_shapes, *proj.sample_shapes, *cast_shapes],
        scratch_shapes=[pltpu.VMEM((2, tm, d), F32), pltpu.VMEM((ts, d), F32)],
        compiler_params=_params("arbitrary"),
        name="ffn_ln",
    )(x_p, x_s, *mix_args, *weights, ln_g, ln_b, *proj.weights, *cast_w)
    return outs


def _cast_slab_rows(rows, steps):
    slab = BF16_SUBLANES
    while rows % slab or rows // slab > steps:
        slab += BF16_SUBLANES
    return slab


class _Projection(NamedTuple):
    weights: tuple
    weight_specs: tuple
    prompt_shapes: tuple
    prompt_specs: tuple
    sample_shapes: tuple
    sample_specs: tuple
    prompt_fn: object
    sample_fn: object


def _q_projection(wq, *, index, scale):
    def store_q(xn, w_refs, out_refs):
        out_refs[0][...] = (_dot(xn.astype(BF16), w_refs[0][...]) * scale).astype(BF16)

    def build(n_prompt, n_sample, d, tm, out_tile):
        return _Projection(
            (wq,), (_const_spec((d, d), (index,)),),
            (jax.ShapeDtypeStruct((n_prompt, d), BF16),), (pl.BlockSpec((tm, d), lambda s: (out_tile(s), 0)),),
            (jax.ShapeDtypeStruct((n_sample, d), BF16),), (pl.BlockSpec((n_sample, d), lambda s: (0, 0)),),
            store_q, store_q)
    return build


def _store_heads(ref, value):
    head_dim = ref.shape[2]
    for h in range(ref.shape[1]):
        ref[:, h, :] = value[:, h * head_dim:(h + 1) * head_dim]


def _kv_projection(w_kv, *, batch, head_dim):
    def kv(xn, w_refs):
        xb = xn.astype(BF16)
        return _dot(xb, w_refs[0][...]), _dot(xb, w_refs[1][...])

    def store_prompt(xn, w_refs, out_refs):
        kpt_ref, vpt_ref, kt_ref, vb_ref = out_refs
        k, v = kv(xn, w_refs)
        vb_ref[...] = v.astype(BF16)
        for hp in range(kt_ref.shape[1]):
            lanes = slice(hp * V7X_LANES, (hp + 1) * V7X_LANES)
            k_t = k[:, lanes].T
            kpt_ref[0, lanes, :] = k_t
            for c in range(kt_ref.shape[2]):
                kt_ref[0, hp, c] = k_t[:, c * TK:(c + 1) * TK].astype(BF16)
            vpt_ref[0, lanes, :] = v[:, lanes].T

    def store_sample(xn, w_refs, out_refs):
        ks_ref, vs_ref, ksf_ref, vsf_ref = out_refs
        k, v = kv(xn, w_refs)
        _store_heads(ks_ref, k)
        _store_heads(vs_ref, v)
        ksf_ref[...] = k
        vsf_ref[...] = v

    def build(n_prompt, n_sample, d, tm, out_tile):
        seq = n_prompt // batch
        tiles_per_stream = seq // tm
        blocks_per_tile = tm // TK
        n_hp = d // V7X_LANES
        heads = (d // head_dim, head_dim)
        where = lambda s: (out_tile(s) // tiles_per_stream, out_tile(s) % tiles_per_stream)
        prompt_t = pl.BlockSpec((1, d, tm), lambda s: (where(s)[0], 0, where(s)[1]))
        kt_spec = pl.BlockSpec((1, n_hp, blocks_per_tile, V7X_LANES, TK),
                               lambda s: (where(s)[0], 0, where(s)[1], 0, 0))
        sample4 = pl.BlockSpec((n_sample, *heads), lambda s: (0, 0, 0))
        sample2 = pl.BlockSpec((n_sample, d), lambda s: (0, 0))
        w_spec = lambda half: pl.BlockSpec((d, d), lambda s: (0, half), pipeline_mode=pl.Buffered(1))
        return _Projection(
            (w_kv, w_kv), (w_spec(0), w_spec(1)),
            (jax.ShapeDtypeStruct((batch, d, seq), F32), jax.ShapeDtypeStruct((batch, d, seq), F32),
             jax.ShapeDtypeStruct((batch, n_hp, seq // TK, V7X_LANES, TK), BF16),
             jax.ShapeDtypeStruct((n_prompt, d), BF16)),
            (prompt_t, prompt_t, kt_spec, pl.BlockSpec((tm, d), lambda s: (out_tile(s), 0))),
            (jax.ShapeDtypeStruct((n_sample, *heads), F32), jax.ShapeDtypeStruct((n_sample, *heads), F32),
             jax.ShapeDtypeStruct((n_sample, d), F32), jax.ShapeDtypeStruct((n_sample, d), F32)),
            (sample4, sample4, sample2, sample2),
            store_prompt, store_sample)
    return build


def _conv_ln_kernel(x_ref, prev_ref, win_ref, wconv_ref, wout_ref, g_ref, b_ref,
                    o_ref, st_ref, car_ref, *, alpha, layer, ln_row):
    tm, d = x_ref.shape
    n_str = car_ref.shape[0]
    t_len = tm // n_str

    @pl.when(pl.program_id(1) == 0)
    def _():
        car_ref[...] = prev_ref[...]

    def per_row(a):
        return a if n_str == 1 else jnp.broadcast_to(a[:, None, :], (n_str, t_len, d)).reshape(tm, d)

    x = x_ref[...]
    p = _dot(x.astype(BF16), win_ref[...])
    gate_out = p[:, :d]
    u = p[:, d:2 * d] * p[:, 2 * d:]
    prev2 = per_row(car_ref[:, 0, :])
    prev1 = per_row(car_ref[:, 1, :])
    t = lax.broadcasted_iota(jnp.int32, (tm, d), 0) % t_len
    u1 = jnp.where(t == 0, prev1, pltpu.roll(u, 1, 0))
    u2 = jnp.where(t == 0, prev2, jnp.where(t == 1, prev1, pltpu.roll(u, 2, 0)))
    wc = wconv_ref[layer]
    conv = wc[0:1, :] * u2 + wc[1:2, :] * u1 + wc[2:3, :] * u
    y = _dot((gate_out * conv).astype(BF16), wout_ref[...])
    o_ref[...] = _layer_norm(alpha * x + y, g_ref[ln_row:ln_row + 1, :], b_ref[ln_row:ln_row + 1, :])
    last = u.reshape(n_str, t_len, d)[:, t_len - (CONV_W - 1):, :]
    car_ref[...] = last
    st_ref[...] = last


def _conv_ln(x, prev, win, wconv, wout, ln_g, ln_b, *, layer, ln_row, alpha, streams, tm, streams_per_block=1):
    n, d = x.shape
    nt = n // streams // tm
    assert streams_per_block == 1 or nt == 1
    x_spec = pl.BlockSpec((tm * streams_per_block, d), lambda s, t: (s * nt + t, 0))
    st_spec = pl.BlockSpec((streams_per_block, CONV_W - 1, d), lambda s, t: (s, 0, 0))
    return pl.pallas_call(
        functools.partial(_conv_ln_kernel, alpha=alpha, layer=layer, ln_row=ln_row),
        grid=(streams // streams_per_block, nt),
        in_specs=[x_spec, st_spec,
                  _const_spec((d, 3 * d), (layer,)), _const_spec(wconv.shape), _const_spec((d, d), (layer,)),
                  _const_spec(ln_g.shape), _const_spec(ln_b.shape)],
        out_specs=[x_spec, st_spec],
        out_shape=[jax.ShapeDtypeStruct((n, d), F32),
                   jax.ShapeDtypeStruct((streams, CONV_W - 1, d), F32)],
        scratch_shapes=[pltpu.VMEM((streams_per_block, CONV_W - 1, d), F32)],
        compiler_params=_params("arbitrary", "arbitrary"),
        name="conv_ln",
    )(x, prev, win, wconv, wout, ln_g, ln_b)


def _sb_block(z, v_blk, umat, acc_ref, car_ref, vis=None, carry_bias=None, pv=_dot):
    sub = V7X_LANES
    n_sub = z.shape[1] // sub
    softplus = jnp.maximum(z, jnp.log(1.0 + jnp.exp(jnp.minimum(z, SOFTPLUS_LINEAR))))
    if vis is not None:
        softplus = jnp.where(vis, softplus, 0.0)
    hi = softplus.astype(BF16)
    lo = (softplus - hi.astype(F32)).astype(BF16)
    run = car_ref[...]
    if carry_bias is not None:
        run = run + carry_bias
    args = [None] * n_sub
    for c in reversed(range(n_sub)):
        cols = slice(c * sub, (c + 1) * sub)
        sums = _dot(jnp.concatenate([hi[:, cols], lo[:, cols]], axis=1), umat)
        args[c] = z[:, cols] + sums[:, :sub] + run
        run = run + sums[:, sub:]
    w = jnp.exp(jnp.concatenate(args, axis=1))
    if vis is not None:
        w = jnp.where(vis, w, 0.0)
    acc_ref[...] += pv(w.astype(BF16), v_blk)
    car_ref[...] = run
    return jnp.max(run)


def _suffix_sum_matrix():
    sub = V7X_LANES
    j = jnp.arange(2 * sub)[:, None] % sub
    s = jnp.arange(2 * sub)[None, :]
    return jnp.where((s >= sub) | (j >= s), -1.0, 0.0).astype(BF16)


def _sb_prompt_kernel(q_ref, kt_ref, v_ref, u_ref, o_ref, acc_ref, car_ref, *, head_dim):
    i = pl.program_id(2)
    tq = q_ref.shape[0]
    tk = kt_ref.shape[-1]
    n_pairs = kt_ref.shape[1]
    lane = lax.broadcasted_iota(jnp.int32, (tq, V7X_LANES), 1)
    row_q = lax.broadcasted_iota(jnp.int32, (2 * tq, tk), 0) & (tq - 1)
    col = lax.broadcasted_iota(jnp.int32, (2 * tq, tk), 1)
    diagonal = col < row_q
    acc_ref[...] = jnp.zeros_like(acc_ref)
    car_ref[...] = jnp.zeros_like(car_ref)
    umat = u_ref[...]
    qm = []
    for p in range(n_pairs):
        q = q_ref[:, p * V7X_LANES:(p + 1) * V7X_LANES]
        zero = jnp.zeros_like(q)
        qm.append(jnp.concatenate([jnp.where(lane < head_dim, q, zero), jnp.where(lane >= head_dim, q, zero)],
                                  axis=0))

    def block(j, vis=None, carry_bias=None):
        cmax = None
        for p in range(n_pairs):
            z = _dot(qm[p], kt_ref[0, p, j])
            v_blk = v_ref[0, j, :, p * V7X_LANES:(p + 1) * V7X_LANES]
            c = _sb_block(z, v_blk, umat, acc_ref.at[p], car_ref.at[p], vis, carry_bias)
            cmax = c if cmax is None else jnp.maximum(cmax, c)
        return cmax

    block(i, vis=diagonal)
    cmax = block(jnp.maximum(i - 1, 0), carry_bias=jnp.where(i > 0, 0.0, NO_BLOCK_BIAS))
    lax.while_loop(lambda s: (s[0] >= 0) & (s[1] > NEG_CUT),
                   lambda s: (s[0] - 1, block(s[0])), (i - 2, cmax))
    for p in range(n_pairs):
        acc = acc_ref[p]
        o_ref[:, p * V7X_LANES:(p + 1) * V7X_LANES] = jnp.where(lane < head_dim, acc[:tq], acc[tq:]).astype(o_ref.dtype)


def _sb_prompt(q, kt, vb, umat, *, batch, head_dim):
    n, d = q.shape
    assert TQ == TK
    seq = n // batch
    n_hp = d // V7X_LANES
    nq = seq // TQ
    nk = seq // TK
    v4 = vb.reshape(batch, nk, TK, d)
    lanes = HP_PER_STEP * V7X_LANES
    return pl.pallas_call(
        functools.partial(_sb_prompt_kernel, head_dim=head_dim),
        grid=(batch, n_hp // HP_PER_STEP, nq),
        in_specs=[pl.BlockSpec((TQ, lanes), lambda b, h, i: (b * nq + i, h)),
                  pl.BlockSpec((1, HP_PER_STEP, nk, V7X_LANES, TK), lambda b, h, i: (b, h, 0, 0, 0)),
                  pl.BlockSpec((1, nk, TK, lanes), lambda b, h, i: (b, 0, 0, h)),
                  _const_spec(umat.shape)],
        out_specs=pl.BlockSpec((TQ, lanes), lambda b, h, i: (b * nq + i, h)),
        out_shape=jax.ShapeDtypeStruct((n, d), BF16),
        scratch_shapes=[pltpu.VMEM((HP_PER_STEP, 2 * TQ, V7X_LANES), F32),
                        pltpu.VMEM((HP_PER_STEP, 2 * TQ, V7X_LANES), F32)],
        compiler_params=_params("arbitrary", "arbitrary", "arbitrary"),
        name="sb_prompt",
    )(q, kt, v4, umat)


def _sb_sample_kernel(q_ref, kn_ref, vn_ref, u_ref, ck_hbm, cv_hbm, o_ref,
                      acc_ref, car_ref, kpad_ref, vpad_ref, kbuf, vbuf, sem, *, head_dim):
    s = pl.program_id(0)
    n_streams = pl.num_programs(0)
    t_new, d = q_ref.shape[1:]
    n_heads = d // head_dim
    rows = n_heads * t_new
    newest = ck_hbm.shape[2] // TK - 1
    slot = s % 2

    def copies(stream, j, into):
        keys = pl.ds(j * TK, TK)
        return (pltpu.make_async_copy(ck_hbm.at[stream, :, keys], kbuf.at[into], sem.at[0, into]),
                pltpu.make_async_copy(cv_hbm.at[stream, :, keys], vbuf.at[into], sem.at[1, into]))

    def start(stream, j, into):
        for c in copies(stream, j, into):
            c.start()

    def wait(stream, j, into):
        for c in copies(stream, j, into):
            c.wait()

    @pl.when(s == 0)
    def _():
        start(s, newest, slot)

    @pl.when(s + 1 < n_streams)
    def _():
        start(s + 1, newest, 1 - slot)

    q = q_ref[0]
    q_rep = jnp.concatenate([q] * n_heads, axis=0)
    row_head = lax.broadcasted_iota(jnp.int32, (rows, d), 0) // t_new
    lane_head = lax.broadcasted_iota(jnp.int32, (rows, d), 1) // head_dim
    qm = jnp.where(row_head == lane_head, q_rep, jnp.zeros_like(q_rep))
    acc_ref[...] = jnp.zeros_like(acc_ref)
    car_ref[...] = jnp.zeros_like(car_ref)
    umat = u_ref[...]

    contract_last = (((1,), (1,)), ((), ()))

    def cache_block():
        z = _dot(qm, kbuf[slot].astype(BF16))
        pv = lambda w, v_t: lax.dot_general(w, v_t, contract_last, preferred_element_type=F32)
        return _sb_block(z, vbuf[slot].astype(BF16), umat, acc_ref, car_ref, pv=pv)

    kpad_ref[...] = jnp.zeros_like(kpad_ref)
    vpad_ref[...] = jnp.zeros_like(vpad_ref)
    kpad_ref[0:t_new, :] = kn_ref[0].astype(BF16)
    vpad_ref[0:t_new, :] = vn_ref[0].astype(BF16)
    n_pad = kpad_ref.shape[0]
    row_q = lax.broadcasted_iota(jnp.int32, (rows, n_pad), 0) % t_new
    col = lax.broadcasted_iota(jnp.int32, (rows, n_pad), 1)
    z_new = lax.dot_general(qm, kpad_ref[...], contract_last, preferred_element_type=F32)
    _sb_block(z_new, vpad_ref[...], umat, acc_ref, car_ref, vis=col < row_q)

    wait(s, newest, slot)
    cmax = cache_block()

    def body(state):
        j, _ = state
        start(s, j, slot)
        wait(s, j, slot)
        return j - 1, cache_block()

    lax.while_loop(lambda st: (st[0] >= 0) & (st[1] > NEG_CUT), body, (newest - 1, cmax))
    acc = acc_ref[...]
    lane_head_o = lax.broadcasted_iota(jnp.int32, (t_new, d), 1) // head_dim
    o = jnp.zeros((t_new, d), F32)
    for h in range(n_heads):
        o = jnp.where(lane_head_o == h, acc[h * t_new:(h + 1) * t_new, :], o)
    o_ref[0] = o.astype(o_ref.dtype)


def _sb_sample(q, k_new, v_new, cache_kt, cache_vt, umat, *, head_dim):
    streams, t_new, d = q.shape
    n_heads = d // head_dim
    assert cache_kt.shape[1] == d and cache_kt.shape[2] % TK == 0
    rows = n_heads * t_new
    new_spec = pl.BlockSpec((1, t_new, d), lambda s: (s, 0, 0))
    hbm_spec = pl.BlockSpec(memory_space=pl.ANY)
    return pl.pallas_call(
        functools.partial(_sb_sample_kernel, head_dim=head_dim),
        grid=(streams,),
        in_specs=[new_spec, new_spec, new_spec, _const_spec(umat.shape), hbm_spec, hbm_spec],
        out_specs=new_spec,
        out_shape=jax.ShapeDtypeStruct((streams, t_new, d), BF16),
        scratch_shapes=[pltpu.VMEM((rows, d), F32), pltpu.VMEM((rows, V7X_LANES), F32),
                        pltpu.VMEM((V7X_LANES, d), BF16), pltpu.VMEM((V7X_LANES, d), BF16),
                        pltpu.VMEM((2, d, TK), F32), pltpu.VMEM((2, d, TK), F32),
                        pltpu.SemaphoreType.DMA((2, 2))],
        compiler_params=_params("arbitrary"),
        name="sb_sample",
    )(q, k_new, v_new, umat, cache_kt, cache_vt)


def kernel(x_prompt, x_sample, cache_k, cache_v, state_conv, ln_g, ln_b, w_ffn_gate, w_ffn_up, w_ffn_down,
           w_conv_in, w_conv, w_conv_out, w_kv, w_q, w_o):
    batch, seq, d = x_prompt.shape
    streams, t_new, _ = x_sample.shape
    depth = ln_g.shape[0]
    n_a = w_conv_in.shape[0]
    head_dim = d // N_HEADS
    alpha = (2.0 * depth) ** 0.25

    x_p = x_prompt.reshape(batch * seq, d)
    x_s = x_sample.reshape(streams * t_new, d)
    zero_prev = jnp.zeros((batch, CONV_W - 1, d), F32)
    past = cache_k.shape[1]
    cache_kt = jnp.transpose(cache_k, (0, 2, 3, 1)).reshape(streams, d, past)
    cache_vt = jnp.transpose(cache_v, (0, 2, 3, 1)).reshape(streams, d, past)
    umat = _suffix_sum_matrix()
    ln_g2 = ln_g.reshape(depth * 3, d)
    ln_b2 = ln_b.reshape(depth * 3, d)
    w_in, w_out = w_conv_in.astype(BF16), w_conv_out.astype(BF16)
    w_kv_b, w_q_b, w_o_b = w_kv.astype(BF16), w_q.astype(BF16), w_o.astype(BF16)
    ffn_stacks = (w_ffn_gate, w_ffn_up, w_ffn_down)
    w_ffn = tuple(w[0, 0].astype(BF16) for w in ffn_stacks)

    def ffn(x_p, x_s, weights, layer, half, mix=None, proj=None):
        nxt = (layer, 1) if half == 0 else (layer + 1, 0)
        cast_next = (*ffn_stacks, *nxt) if nxt[0] < depth else None
        outs = _ffn_ln(x_p, x_s, weights, ln_g2, ln_b2, ln_row=3 * layer + 2 * half, alpha=alpha, tm=TM_FFN,
                       mix=mix, proj=proj, cast_next=cast_next)
        n_keep = len(outs) - (len(ffn_stacks) if cast_next else 0)
        return tuple(outs[:n_keep]), tuple(outs[n_keep:])

    conv_prompt, conv_sample = [], []
    for l in range(depth):
        if l < n_a:
            (x_p, x_s), w_ffn = ffn(x_p, x_s, w_ffn, l, 0)
            conv = functools.partial(_conv_ln, win=w_in, wconv=w_conv, wout=w_out, ln_g=ln_g2, ln_b=ln_b2,
                                     layer=l, ln_row=3 * l + 1, alpha=alpha)
            x_p, st = conv(x_p, zero_prev, streams=batch, tm=TM_CONV)
            conv_prompt.append(st)
            x_s, st = conv(x_s, state_conv[l], streams=streams, tm=t_new, streams_per_block=streams)
            conv_sample.append(st)
        else:
            i = l - n_a
            (x_p, x_s, q_p, q_s), w_ffn = ffn(x_p, x_s, w_ffn, l, 0,
                                              proj=_q_projection(w_q_b, index=i, scale=head_dim ** -0.5))
            o_p = _sb_prompt(q_p, kt, vb, umat, batch=batch, head_dim=head_dim)
            o_s = _sb_sample(q_s.reshape(streams, t_new, d),
                             k_sf.reshape(streams, t_new, d), v_sf.reshape(streams, t_new, d),
                             cache_kt, cache_vt, umat, head_dim=head_dim)
            mix = (o_p, o_s.reshape(streams * t_new, d), w_o_b, i, 3 * l + 1)
        if l == n_a - 1:
            (x_p, x_s, k_pt, v_pt, kt, vb, k_s, v_s, k_sf, v_sf), w_ffn = ffn(
                x_p, x_s, w_ffn, l, 1, proj=_kv_projection(w_kv_b, batch=batch, head_dim=head_dim))
        else:
            (x_p, x_s), w_ffn = ffn(x_p, x_s, w_ffn, l, 1, mix=mix if l >= n_a else None)

    heads = (N_HEADS, head_dim)
    to_bthd = lambda a_t: jnp.transpose(a_t.reshape(batch, *heads, seq), (0, 3, 1, 2))
    return (x_p.reshape(batch, seq, d),
            x_s.reshape(streams, t_new, d),
            to_bthd(k_pt), to_bthd(v_pt),
            jnp.stack(conv_prompt, axis=0),
            k_s.reshape(streams, t_new, *heads), v_s.reshape(streams, t_new, *heads),
            jnp.stack(conv_sample, axis=0))
```

```python
import functools
from typing import NamedTuple

import jax
import jax.numpy as jnp
from jax import lax
from jax.experimental import pallas as pl
from jax.experimental.pallas import tpu as pltpu

F32 = jnp.float32
BF16 = jnp.bfloat16

N_HEADS = 16
CONV_W = 3
LN_EPS = 1e-5

V7X_LANES = 128
BF16_SUBLANES = 16
VMEM_LIMIT_BYTES = 56 * 1024 * 1024

TM_FFN = 256
HP_PER_STEP = 4
TQ = 256
TK = 256
TM_CONV = 1024

NEG_CUT = -110.0
NO_BLOCK_BIAS = -1e30
SOFTPLUS_LINEAR = 80.0


def _dot(a, b):
    return jnp.dot(a, b, preferred_element_type=F32)


def _layer_norm(r, g, b):
    mu = jnp.mean(r, axis=-1, keepdims=True)
    xc = r - mu
    var = jnp.mean(xc * xc, axis=-1, keepdims=True)
    return xc * lax.rsqrt(var + LN_EPS) * g + b


def _params(*sem):
    return pltpu.CompilerParams(dimension_semantics=sem, vmem_limit_bytes=VMEM_LIMIT_BYTES)


def _const_spec(shape, index=None):
    if index is None:
        return pl.BlockSpec(shape, lambda *_: (0,) * len(shape), pipeline_mode=pl.Buffered(1))
    block = (None,) * len(index) + tuple(shape)
    full = tuple(index) + (0,) * len(shape)
    return pl.BlockSpec(block, lambda *_: full, pipeline_mode=pl.Buffered(1))


def _ffn_ln_kernel(*refs, alpha, ln_row, mix_ln_row, npt, n_proj_w, n_proj_prompt, n_cast, proj_prompt,
                   proj_sample):
    xp_ref, xs_ref = refs[:2]
    if mix_ln_row is not None:
        mp_ref, ms_ref, wo_ref = refs[2:5]
        refs = refs[:2] + refs[5:]
    wg_ref, wu_ref, wd_ref, g_ref, b_ref = refs[2:7]
    proj_w = refs[7:7 + n_proj_w]
    cast_in = refs[7 + n_proj_w:7 + n_proj_w + n_cast]
    outs = refs[7 + n_proj_w + n_cast:-2]
    op_ref, os_ref = outs[:2]
    prompt_out = outs[2:2 + n_proj_prompt]
    sample_out = outs[2 + n_proj_prompt:len(outs) - n_cast]
    cast_out = outs[len(outs) - n_cast:]
    pre_ref, pre_s_ref = refs[-2:]
    s = pl.program_id(0)

    for src, dst in zip(cast_in, cast_out):
        dst[...] = src[...].astype(dst.dtype)

    def pre_norm(x_ref, mix_ref):
        x = x_ref[...]
        if mix_ln_row is not None:
            x = _layer_norm(alpha * x + _dot(mix_ref[...], wo_ref[...]),
                            g_ref[mix_ln_row:mix_ln_row + 1, :], b_ref[mix_ln_row:mix_ln_row + 1, :])
        xb = x.astype(BF16)
        gate = _dot(xb, wg_ref[...])
        up = _dot(xb, wu_ref[...])
        return alpha * x + 0.5 * _dot((gate * jax.nn.sigmoid(gate) * up).astype(BF16), wd_ref[...])

    def finish(pre, x_ref, proj, out_refs):
        xn = _layer_norm(pre, g_ref[ln_row:ln_row + 1, :], b_ref[ln_row:ln_row + 1, :])
        x_ref[...] = xn
        if proj is not None:
            proj(xn, proj_w, out_refs)

    @pl.when(s == 0)
    def _():
        pre_ref[1] = jnp.zeros(pre_ref.shape[1:], F32)
        pre_s_ref[...] = pre_norm(xs_ref, ms_ref if mix_ln_row is not None else None)

    @pl.when((s >= 1) & (s <= npt))
    def _():
        finish(pre_ref[s % 2], op_ref, proj_prompt, prompt_out)
        pre_ref[(s + 1) % 2] = pre_norm(xp_ref, mp_ref if mix_ln_row is not None else None)

    @pl.when(s == 1)
    def _():
        finish(pre_s_ref[...], os_ref, proj_sample, sample_out)

    @pl.when(s == npt + 1)
    def _():
        finish(pre_ref[(npt + 1) % 2], op_ref, proj_prompt, prompt_out)


def _ffn_ln(x_p, x_s, weights, ln_g, ln_b, *, ln_row, alpha, tm, mix=None, proj=None, casts=()):
    mix_args, mix_specs, mix_ln_row = (), (), None
    if mix is not None:
        *mix_args, mix_ln_row = mix
    n_prompt, d = x_p.shape
    ts = x_s.shape[0]
    dff = weights[0].shape[1]
    npt = n_prompt // tm
    steps = npt + 2
    in_tile = lambda s: jnp.clip(s - 1, 0, npt - 1)
    out_tile = lambda s: jnp.clip(s - 2, 0, npt - 1)
    sample = pl.BlockSpec((ts, d), lambda s: (0, 0))
    prompt_out = pl.BlockSpec((tm, d), lambda s: (out_tile(s), 0))
    prompt_in = pl.BlockSpec((tm, d), lambda s: (in_tile(s), 0))
    if mix is not None:
        mix_specs = (prompt_in, sample, _const_spec((d, d)))
    if proj is None:
        proj = _Projection((), (), (), (), (), (), None, None)
    else:
        proj = proj(n_prompt, ts, d, tm, out_tile)
    cast_w, cast_in_specs, cast_out_specs, cast_shapes = [], [], [], []
    for w, lead in casts:
        rows, cols = w.shape[len(lead):]
        slab = _cast_slab_rows(rows, steps)
        slab_of = functools.partial(lambda s, last: jnp.minimum(s, last), last=rows // slab - 1)
        cast_w.append(w)
        cast_in_specs.append(pl.BlockSpec((None,) * len(lead) + (slab, cols),
                                          lambda s, slab_of=slab_of, lead=lead: (*lead, slab_of(s), 0)))
        cast_out_specs.append(pl.BlockSpec((slab, cols), lambda s, slab_of=slab_of: (slab_of(s), 0)))
        cast_shapes.append(jax.ShapeDtypeStruct((rows, cols), BF16))
    outs = pl.pallas_call(
        functools.partial(_ffn_ln_kernel, alpha=alpha, ln_row=ln_row, mix_ln_row=mix_ln_row, npt=npt,
                          n_proj_w=len(proj.weights), n_proj_prompt=len(proj.prompt_shapes), n_cast=len(cast_w),
                          proj_prompt=proj.prompt_fn, proj_sample=proj.sample_fn),
        grid=(steps,),
        in_specs=[prompt_in, sample, *mix_specs,
                  _const_spec((d, dff)), _const_spec((d, dff)), _const_spec((dff, d)),
                  _const_spec(ln_g.shape), _const_spec(ln_b.shape), *proj.weight_specs, *cast_in_specs],
        out_specs=[prompt_out, sample, *proj.prompt_specs, *proj.sample_specs, *cast_out_specs],
        out_shape=[jax.ShapeDtypeStruct(x_p.shape, F32), jax.ShapeDtypeStruct(x_s.shape, F32),
                   *proj.prompt_shapes, *proj.sample_shapes, *cast_shapes],
        scratch_shapes=[pltpu.VMEM((2, tm, d), F32), pltpu.VMEM((ts, d), F32)],
        compiler_params=_params("arbitrary"),
        name="ffn_ln",
    )(x_p, x_s, *mix_args, *weights, ln_g, ln_b, *proj.weights, *cast_w)
    return outs


def _cast_slab_rows(rows, steps):
    slab = BF16_SUBLANES
    while rows % slab or rows // slab > steps:
        slab += BF16_SUBLANES
    return slab


class _Projection(NamedTuple):
    weights: tuple
    weight_specs: tuple
    prompt_shapes: tuple
    prompt_specs: tuple
    sample_shapes: tuple
    sample_specs: tuple
    prompt_fn: object
    sample_fn: object


def _q_projection(wq, *, scale):
    def store_q(xn, w_refs, out_refs):
        out_refs[0][...] = (_dot(xn.astype(BF16), w_refs[0][...]) * scale).astype(BF16)

    def build(n_prompt, n_sample, d, tm, out_tile):
        return _Projection(
            (wq,), (_const_spec((d, d)),),
            (jax.ShapeDtypeStruct((n_prompt, d), BF16),), (pl.BlockSpec((tm, d), lambda s: (out_tile(s), 0)),),
            (jax.ShapeDtypeStruct((n_sample, d), BF16),), (pl.BlockSpec((n_sample, d), lambda s: (0, 0)),),
            store_q, store_q)
    return build


def _store_heads(ref, value):
    head_dim = ref.shape[2]
    for h in range(ref.shape[1]):
        ref[:, h, :] = value[:, h * head_dim:(h + 1) * head_dim]


def _kv_projection(w_kv, *, batch, head_dim):
    def kv(xn, w_refs):
        xb = xn.astype(BF16)
        return _dot(xb, w_refs[0][...]), _dot(xb, w_refs[1][...])

    def store_prompt(xn, w_refs, out_refs):
        kpt_ref, vpt_ref, kt_ref, vb_ref = out_refs
        k, v = kv(xn, w_refs)
        vb_ref[...] = v.astype(BF16)
        for hp in range(kt_ref.shape[1]):
            lanes = slice(hp * V7X_LANES, (hp + 1) * V7X_LANES)
            k_t = k[:, lanes].T
            kpt_ref[0, lanes, :] = k_t
            for c in range(kt_ref.shape[2]):
                kt_ref[0, hp, c] = k_t[:, c * TK:(c + 1) * TK].astype(BF16)
            vpt_ref[0, lanes, :] = v[:, lanes].T

    def store_sample(xn, w_refs, out_refs):
        ks_ref, vs_ref, ksf_ref, vsf_ref = out_refs
        k, v = kv(xn, w_refs)
        _store_heads(ks_ref, k)
        _store_heads(vs_ref, v)
        ksf_ref[...] = k
        vsf_ref[...] = v

    def build(n_prompt, n_sample, d, tm, out_tile):
        seq = n_prompt // batch
        tiles_per_stream = seq // tm
        blocks_per_tile = tm // TK
        n_hp = d // V7X_LANES
        heads = (d // head_dim, head_dim)
        where = lambda s: (out_tile(s) // tiles_per_stream, out_tile(s) % tiles_per_stream)
        prompt_t = pl.BlockSpec((1, d, tm), lambda s: (where(s)[0], 0, where(s)[1]))
        kt_spec = pl.BlockSpec((1, n_hp, blocks_per_tile, V7X_LANES, TK),
                               lambda s: (where(s)[0], 0, where(s)[1], 0, 0))
        sample4 = pl.BlockSpec((n_sample, *heads), lambda s: (0, 0, 0))
        sample2 = pl.BlockSpec((n_sample, d), lambda s: (0, 0))
        w_spec = lambda half: pl.BlockSpec((d, d), lambda s: (0, half), pipeline_mode=pl.Buffered(1))
        return _Projection(
            (w_kv, w_kv), (w_spec(0), w_spec(1)),
            (jax.ShapeDtypeStruct((batch, d, seq), F32), jax.ShapeDtypeStruct((batch, d, seq), F32),
             jax.ShapeDtypeStruct((batch, n_hp, seq // TK, V7X_LANES, TK), BF16),
             jax.ShapeDtypeStruct((n_prompt, d), BF16)),
            (prompt_t, prompt_t, kt_spec, pl.BlockSpec((tm, d), lambda s: (out_tile(s), 0))),
            (jax.ShapeDtypeStruct((n_sample, *heads), F32), jax.ShapeDtypeStruct((n_sample, *heads), F32),
             jax.ShapeDtypeStruct((n_sample, d), F32), jax.ShapeDtypeStruct((n_sample, d), F32)),
            (sample4, sample4, sample2, sample2),
            store_prompt, store_sample)
    return build


def _conv_ln_kernel(x_ref, prev_ref, win_ref, wconv_ref, wout_ref, g_ref, b_ref,
                    o_ref, st_ref, car_ref, *, alpha, ln_row):
    tm, d = x_ref.shape
    n_str = car_ref.shape[0]
    t_len = tm // n_str

    @pl.when(pl.program_id(1) == 0)
    def _():
        car_ref[...] = prev_ref[...]

    def per_row(a):
        return a if n_str == 1 else jnp.broadcast_to(a[:, None, :], (n_str, t_len, d)).reshape(tm, d)

    x = x_ref[...]
    p = _dot(x.astype(BF16), win_ref[...])
    gate_out = p[:, :d]
    u = p[:, d:2 * d] * p[:, 2 * d:]
    prev2 = per_row(car_ref[:, 0, :])
    prev1 = per_row(car_ref[:, 1, :])
    t = lax.broadcasted_iota(jnp.int32, (tm, d), 0) % t_len
    u1 = jnp.where(t == 0, prev1, pltpu.roll(u, 1, 0))
    u2 = jnp.where(t == 0, prev2, jnp.where(t == 1, prev1, pltpu.roll(u, 2, 0)))
    wc = wconv_ref[...]
    conv = wc[0:1, :] * u2 + wc[1:2, :] * u1 + wc[2:3, :] * u
    y = _dot((gate_out * conv).astype(BF16), wout_ref[...])
    o_ref[...] = _layer_norm(alpha * x + y, g_ref[ln_row:ln_row + 1, :], b_ref[ln_row:ln_row + 1, :])
    last = u.reshape(n_str, t_len, d)[:, t_len - (CONV_W - 1):, :]
    car_ref[...] = last
    st_ref[...] = last


def _conv_ln(x, prev, win, wconv, wout, ln_g, ln_b, *, layer, ln_row, alpha, streams, tm, streams_per_block=1):
    n, d = x.shape
    nt = n // streams // tm
    assert streams_per_block == 1 or nt == 1
    x_spec = pl.BlockSpec((tm * streams_per_block, d), lambda s, t: (s * nt + t, 0))
    st_spec = pl.BlockSpec((streams_per_block, CONV_W - 1, d), lambda s, t: (s, 0, 0))
    return pl.pallas_call(
        functools.partial(_conv_ln_kernel, alpha=alpha, ln_row=ln_row),
        grid=(streams // streams_per_block, nt),
        in_specs=[x_spec, st_spec,
                  _const_spec((d, 3 * d)), _const_spec(wconv.shape[1:], (layer,)), _const_spec((d, d)),
                  _const_spec(ln_g.shape), _const_spec(ln_b.shape)],
        out_specs=[x_spec, st_spec],
        out_shape=[jax.ShapeDtypeStruct((n, d), F32),
                   jax.ShapeDtypeStruct((streams, CONV_W - 1, d), F32)],
        scratch_shapes=[pltpu.VMEM((streams_per_block, CONV_W - 1, d), F32)],
        compiler_params=_params("arbitrary", "arbitrary"),
        name="conv_ln",
    )(x, prev, win, wconv, wout, ln_g, ln_b)


def _sb_block(z, v_blk, umat, acc_ref, car_ref, vis=None, carry_bias=None, pv=_dot):
    sub = V7X_LANES
    n_sub = z.shape[1] // sub
    softplus = jnp.maximum(z, jnp.log(1.0 + jnp.exp(jnp.minimum(z, SOFTPLUS_LINEAR))))
    if vis is not None:
        softplus = jnp.where(vis, softplus, 0.0)
    hi = softplus.astype(BF16)
    lo = (softplus - hi.astype(F32)).astype(BF16)
    run = car_ref[...]
    if carry_bias is not None:
        run = run + carry_bias
    args = [None] * n_sub
    for c in reversed(range(n_sub)):
        cols = slice(c * sub, (c + 1) * sub)
        sums = _dot(jnp.concatenate([hi[:, cols], lo[:, cols]], axis=1), umat)
        args[c] = z[:, cols] + sums[:, :sub] + run
        run = run + sums[:, sub:]
    w = jnp.exp(jnp.concatenate(args, axis=1))
    if vis is not None:
        w = jnp.where(vis, w, 0.0)
    acc_ref[...] += pv(w.astype(BF16), v_blk)
    car_ref[...] = run
    return jnp.max(run)


def _suffix_sum_matrix():
    sub = V7X_LANES
    j = jnp.arange(2 * sub)[:, None] % sub
    s = jnp.arange(2 * sub)[None, :]
    return jnp.where((s >= sub) | (j >= s), -1.0, 0.0).astype(BF16)


def _sb_prompt_kernel(q_ref, kt_ref, v_ref, u_ref, o_ref, acc_ref, car_ref, *, head_dim):
    i = pl.program_id(2)
    tq = q_ref.shape[0]
    tk = kt_ref.shape[-1]
    n_pairs = kt_ref.shape[1]
    lane = lax.broadcasted_iota(jnp.int32, (tq, V7X_LANES), 1)
    row_q = lax.broadcasted_iota(jnp.int32, (2 * tq, tk), 0) & (tq - 1)
    col = lax.broadcasted_iota(jnp.int32, (2 * tq, tk), 1)
    diagonal = col < row_q
    acc_ref[...] = jnp.zeros_like(acc_ref)
    car_ref[...] = jnp.zeros_like(car_ref)
    umat = u_ref[...]
    qm = []
    for p in range(n_pairs):
        q = q_ref[:, p * V7X_LANES:(p + 1) * V7X_LANES]
        zero = jnp.zeros_like(q)
        qm.append(jnp.concatenate([jnp.where(lane < head_dim, q, zero), jnp.where(lane >= head_dim, q, zero)],
                                  axis=0))

    def block(j, vis=None, carry_bias=None):
        cmax = None
        for p in range(n_pairs):
            z = _dot(qm[p], kt_ref[0, p, j])
            v_blk = v_ref[0, j, :, p * V7X_LANES:(p + 1) * V7X_LANES]
            c = _sb_block(z, v_blk, umat, acc_ref.at[p], car_ref.at[p], vis, carry_bias)
            cmax = c if cmax is None else jnp.maximum(cmax, c)
        return cmax

    block(i, vis=diagonal)
    cmax = block(jnp.maximum(i - 1, 0), carry_bias=jnp.where(i > 0, 0.0, NO_BLOCK_BIAS))
    lax.while_loop(lambda s: (s[0] >= 0) & (s[1] > NEG_CUT),
                   lambda s: (s[0] - 1, block(s[0])), (i - 2, cmax))
    for p in range(n_pairs):
        acc = acc_ref[p]
        o_ref[:, p * V7X_LANES:(p + 1) * V7X_LANES] = jnp.where(lane < head_dim, acc[:tq], acc[tq:]).astype(o_ref.dtype)


def _sb_prompt(q, kt, vb, umat, *, batch, head_dim):
    n, d = q.shape
    assert TQ == TK
    seq = n // batch
    n_hp = d // V7X_LANES
    nq = seq // TQ
    nk = seq // TK
    v4 = vb.reshape(batch, nk, TK, d)
    lanes = HP_PER_STEP * V7X_LANES
    return pl.pallas_call(
        functools.partial(_sb_prompt_kernel, head_dim=head_dim),
        grid=(batch, n_hp // HP_PER_STEP, nq),
        in_specs=[pl.BlockSpec((TQ, lanes), lambda b, h, i: (b * nq + i, h)),
                  pl.BlockSpec((1, HP_PER_STEP, nk, V7X_LANES, TK), lambda b, h, i: (b, h, 0, 0, 0)),
                  pl.BlockSpec((1, nk, TK, lanes), lambda b, h, i: (b, 0, 0, h)),
                  _const_spec(umat.shape)],
        out_specs=pl.BlockSpec((TQ, lanes), lambda b, h, i: (b * nq + i, h)),
        out_shape=jax.ShapeDtypeStruct((n, d), BF16),
        scratch_shapes=[pltpu.VMEM((HP_PER_STEP, 2 * TQ, V7X_LANES), F32),
                        pltpu.VMEM((HP_PER_STEP, 2 * TQ, V7X_LANES), F32)],
        compiler_params=_params("arbitrary", "arbitrary", "arbitrary"),
        name="sb_prompt",
    )(q, kt, v4, umat)


def _sb_sample_kernel(q_ref, kn_ref, vn_ref, u_ref, ck_hbm, cv_hbm, o_ref,
                      acc_ref, car_ref, kpad_ref, vpad_ref, kbuf, vbuf, sem, *, head_dim):
    s = pl.program_id(0)
    n_streams = pl.num_programs(0)
    t_new, d = q_ref.shape[1:]
    n_heads = d // head_dim
    rows = n_heads * t_new
    newest = ck_hbm.shape[2] // TK - 1
    slot = s % 2

    def copies(stream, j, into):
        keys = pl.ds(j * TK, TK)
        return (pltpu.make_async_copy(ck_hbm.at[stream, :, keys], kbuf.at[into], sem.at[0, into]),
                pltpu.make_async_copy(cv_hbm.at[stream, :, keys], vbuf.at[into], sem.at[1, into]))

    def start(stream, j, into):
        for c in copies(stream, j, into):
            c.start()

    def wait(stream, j, into):
        for c in copies(stream, j, into):
            c.wait()

    @pl.when(s == 0)
    def _():
        start(s, newest, slot)

    @pl.when(s + 1 < n_streams)
    def _():
        start(s + 1, newest, 1 - slot)

    q = q_ref[0]
    q_rep = jnp.concatenate([q] * n_heads, axis=0)
    row_head = lax.broadcasted_iota(jnp.int32, (rows, d), 0) // t_new
    lane_head = lax.broadcasted_iota(jnp.int32, (rows, d), 1) // head_dim
    qm = jnp.where(row_head == lane_head, q_rep, jnp.zeros_like(q_rep))
    acc_ref[...] = jnp.zeros_like(acc_ref)
    car_ref[...] = jnp.zeros_like(car_ref)
    umat = u_ref[...]

    contract_last = (((1,), (1,)), ((), ()))

    def cache_block():
        z = _dot(qm, kbuf[slot].astype(BF16))
        pv = lambda w, v_t: lax.dot_general(w, v_t, contract_last, preferred_element_type=F32)
        return _sb_block(z, vbuf[slot].astype(BF16), umat, acc_ref, car_ref, pv=pv)

    kpad_ref[...] = jnp.zeros_like(kpad_ref)
    vpad_ref[...] = jnp.zeros_like(vpad_ref)
    kpad_ref[0:t_new, :] = kn_ref[0].astype(BF16)
    vpad_ref[0:t_new, :] = vn_ref[0].astype(BF16)
    n_pad = kpad_ref.shape[0]
    row_q = lax.broadcasted_iota(jnp.int32, (rows, n_pad), 0) % t_new
    col = lax.broadcasted_iota(jnp.int32, (rows, n_pad), 1)
    z_new = lax.dot_general(qm, kpad_ref[...], contract_last, preferred_element_type=F32)
    _sb_block(z_new, vpad_ref[...], umat, acc_ref, car_ref, vis=col < row_q)

    wait(s, newest, slot)
    cmax = cache_block()

    def body(state):
        j, _ = state
        start(s, j, slot)
        wait(s, j, slot)
        return j - 1, cache_block()

    lax.while_loop(lambda st: (st[0] >= 0) & (st[1] > NEG_CUT), body, (newest - 1, cmax))
    acc = acc_ref[...]
    lane_head_o = lax.broadcasted_iota(jnp.int32, (t_new, d), 1) // head_dim
    o = jnp.zeros((t_new, d), F32)
    for h in range(n_heads):
        o = jnp.where(lane_head_o == h, acc[h * t_new:(h + 1) * t_new, :], o)
    o_ref[0] = o.astype(o_ref.dtype)


def _sb_sample(q, k_new, v_new, cache_kt, cache_vt, umat, *, head_dim):
    streams, t_new, d = q.shape
    n_heads = d // head_dim
    assert cache_kt.shape[1] == d and cache_kt.shape[2] % TK == 0
    rows = n_heads * t_new
    new_spec = pl.BlockSpec((1, t_new, d), lambda s: (s, 0, 0))
    hbm_spec = pl.BlockSpec(memory_space=pl.ANY)
    return pl.pallas_call(
        functools.partial(_sb_sample_kernel, head_dim=head_dim),
        grid=(streams,),
        in_specs=[new_spec, new_spec, new_spec, _const_spec(umat.shape), hbm_spec, hbm_spec],
        out_specs=new_spec,
        out_shape=jax.ShapeDtypeStruct((streams, t_new, d), BF16),
        scratch_shapes=[pltpu.VMEM((rows, d), F32), pltpu.VMEM((rows, V7X_LANES), F32),
                        pltpu.VMEM((V7X_LANES, d), BF16), pltpu.VMEM((V7X_LANES, d), BF16),
                        pltpu.VMEM((2, d, TK), F32), pltpu.VMEM((2, d, TK), F32),
                        pltpu.SemaphoreType.DMA((2, 2))],
        compiler_params=_params("arbitrary"),
        name="sb_sample",
    )(q, k_new, v_new, umat, cache_kt, cache_vt)


def kernel(x_prompt, x_sample, cache_k, cache_v, state_conv, ln_g, ln_b, w_ffn_gate, w_ffn_up, w_ffn_down,
           w_conv_in, w_conv, w_conv_out, w_kv, w_q, w_o):
    batch, seq, d = x_prompt.shape
    streams, t_new, _ = x_sample.shape
    depth = ln_g.shape[0]
    n_a = w_conv_in.shape[0]
    head_dim = d // N_HEADS
    alpha = (2.0 * depth) ** 0.25

    x_p = x_prompt.reshape(batch * seq, d)
    x_s = x_sample.reshape(streams * t_new, d)
    zero_prev = jnp.zeros((batch, CONV_W - 1, d), F32)
    past = cache_k.shape[1]
    cache_kt = jnp.transpose(cache_k, (0, 2, 3, 1)).reshape(streams, d, past)
    cache_vt = jnp.transpose(cache_v, (0, 2, 3, 1)).reshape(streams, d, past)
    umat = _suffix_sum_matrix()
    ln_g2 = ln_g.reshape(depth * 3, d)
    ln_b2 = ln_b.reshape(depth * 3, d)
    ffn_stacks = {"gate": w_ffn_gate, "up": w_ffn_up, "down": w_ffn_down}
    wb = {name: w[0, 0].astype(BF16) for name, w in ffn_stacks.items()}

    def casts_of(layer, half):
        jobs = {}
        if half == 0 and layer < n_a:
            jobs["conv_in"], jobs["conv_out"] = (w_conv_in, (layer,)), (w_conv_out, (layer,))
        nxt = (layer, 1) if half == 0 else (layer + 1, 0)
        if nxt[0] < depth:
            jobs.update({name: (w, nxt) for name, w in ffn_stacks.items()})
            if nxt == (n_a - 1, 1):
                jobs["kv"] = (w_kv, ())
            elif nxt[0] >= n_a:
                jobs["q" if nxt[1] == 0 else "o"] = ((w_q, w_o)[nxt[1]], (nxt[0] - n_a,))
        return jobs

    def ffn(x_p, x_s, layer, half, mix=None, proj=None):
        jobs = casts_of(layer, half)
        outs = _ffn_ln(x_p, x_s, (wb["gate"], wb["up"], wb["down"]), ln_g2, ln_b2, ln_row=3 * layer + 2 * half,
                       alpha=alpha, tm=TM_FFN, mix=mix, proj=proj, casts=tuple(jobs.values()))
        n_keep = len(outs) - len(jobs)
        wb.update(zip(jobs, outs[n_keep:]))
        return outs[:n_keep]

    conv_prompt, conv_sample = [], []
    for l in range(depth):
        if l < n_a:
            x_p, x_s = ffn(x_p, x_s, l, 0)
            conv = functools.partial(_conv_ln, win=wb["conv_in"], wconv=w_conv, wout=wb["conv_out"],
                                     ln_g=ln_g2, ln_b=ln_b2, layer=l, ln_row=3 * l + 1, alpha=alpha)
            x_p, st = conv(x_p, zero_prev, streams=batch, tm=TM_CONV)
            conv_prompt.append(st)
            x_s, st = conv(x_s, state_conv[l], streams=streams, tm=t_new, streams_per_block=streams)
            conv_sample.append(st)
        else:
            x_p, x_s, q_p, q_s = ffn(x_p, x_s, l, 0, proj=_q_projection(wb["q"], scale=head_dim ** -0.5))
            o_p = _sb_prompt(q_p, kt, vb, umat, batch=batch, head_dim=head_dim)
            o_s = _sb_sample(q_s.reshape(streams, t_new, d),
                             k_sf.reshape(streams, t_new, d), v_sf.reshape(streams, t_new, d),
                             cache_kt, cache_vt, umat, head_dim=head_dim)
            mix = (o_p, o_s.reshape(streams * t_new, d), wb["o"], 3 * l + 1)
        if l == n_a - 1:
            x_p, x_s, k_pt, v_pt, kt, vb, k_s, v_s, k_sf, v_sf = ffn(
                x_p, x_s, l, 1, proj=_kv_projection(wb["kv"], batch=batch, head_dim=head_dim))
        else:
            x_p, x_s = ffn(x_p, x_s, l, 1, mix=mix if l >= n_a else None)

    heads = (N_HEADS, head_dim)
    to_bthd = lambda a_t: jnp.transpose(a_t.reshape(batch, *heads, seq), (0, 3, 1, 2))
    return (x_p.reshape(batch, seq, d),
            x_s.reshape(streams, t_new, d),
            to_bthd(k_pt), to_bthd(v_pt),
            jnp.stack(conv_prompt, axis=0),
            k_s.reshape(streams, t_new, *heads), v_s.reshape(streams, t_new, *heads),
            jnp.stack(conv_sample, axis=0))
```
